```python
import jax, jax.numpy as jnp
from jax import lax
import numpy as np

D_MODEL = 1024
BATCH = 4
SEQ = 4096
DEPTH = 4

CHUNK = 64
N_MIXERS = 4
N_HEADS = 16
HEAD_DIM = D_MODEL // N_HEADS
SB_BLOCK = 128
GMLP_BLOCK = 128
GMLP_WIDTH = 2 * D_MODEL
GMLP_GROUPS = 8
CONV_WIDTH = 3
BAND_CHUNKS = 8
REL_CLIP = 128
D_FF = -(-8 * D_MODEL // (3 * 256)) * 256
EPS = 1e-6

kernel_name = "hybrid_chunk_causal_trunk"


def rms_norm(x, g):
    xf = x.astype(jnp.float32)
    y = xf * lax.rsqrt(jnp.mean(xf * xf, axis=-1, keepdims=True) + EPS)
    return (y * g.astype(jnp.float32)).astype(x.dtype)


def layer_norm(x, g):
    xf = x.astype(jnp.float32)
    mu = jnp.mean(xf, axis=-1, keepdims=True)
    xc = xf - mu
    var = jnp.mean(xc * xc, axis=-1, keepdims=True)
    return (xc * lax.rsqrt(var + EPS) * g.astype(jnp.float32)).astype(x.dtype)


def modulate(h, shift, scale):
    return h * (1 + scale) + shift


def stick_breaking_mixer(h, w_qkv, w_o):
    b, s, _ = h.shape
    qkv = (h @ w_qkv).reshape(b, s, 3, N_HEADS, HEAD_DIM)
    q, k, v = qkv[:, :, 0], qkv[:, :, 1], qkv[:, :, 2]
    scale = HEAD_DIM ** -0.5
    outs = []
    for q0 in range(0, s, SB_BLOCK):
        end = q0 + SB_BLOCK
        z = jnp.einsum('bthd,bshd->bhts', q[:, q0:end], k[:, :end]).astype(jnp.float32) * scale
        t_pos = q0 + jnp.arange(SB_BLOCK)[:, None]
        s_pos = jnp.arange(end)[None, :]
        mask = s_pos < t_pos
        sp = jnp.where(mask, jax.nn.softplus(z), 0.0)
        rest = lax.cumsum(sp, axis=3, reverse=True)
        log_a = jnp.where(mask, z - rest, -jnp.inf)
        a = jnp.exp(log_a).astype(v.dtype)
        outs.append(jnp.einsum('bhts,bshd->bthd', a, v[:, :end]))
    o = jnp.concatenate(outs, axis=1).reshape(b, s, D_MODEL)
    return o @ w_o


def spatial_gating_mixer(h, w_in, ln_g, w_s, s_bias, w_out):
    b, s, _ = h.shape
    z = jax.nn.gelu(h @ w_in)
    u, v = jnp.split(z, 2, axis=-1)
    v = layer_norm(v, ln_g)
    nb = s // GMLP_BLOCK
    gc = GMLP_WIDTH // GMLP_GROUPS
    v = v.reshape(b, nb, GMLP_BLOCK, GMLP_GROUPS, gc)
    pos_chunk = jnp.arange(GMLP_BLOCK) // CHUNK
    mask = pos_chunk[:, None] >= pos_chunk[None, :]
    ws = jnp.where(mask[None], w_s, 0.0).astype(v.dtype)
    sv = jnp.einsum('gij,bnjgc->bnigc', ws, v) + s_bias.T[:, :, None].astype(v.dtype)
    y = u * sv.reshape(b, s, GMLP_WIDTH)
    return y @ w_out


def short_conv_mixer(h, w_in, conv_w, w_out):
    gb, gc, xt = jnp.split(h @ w_in, 3, axis=-1)
    y = gc * xt
    yc = lax.conv_general_dilated(
        y, conv_w.reshape(CONV_WIDTH, 1, D_MODEL).astype(y.dtype),
        window_strides=(1,), padding=[(CONV_WIDTH - 1, 0)],
        dimension_numbers=('NWC', 'WIO', 'NWC'), feature_group_count=D_MODEL)
    return (gb * yc) @ w_out


def chunk_band_attention_mixer(h, w_qkv, rel_bias, w_o):
    b, s, _ = h.shape
    nc = s // CHUNK
    nband = BAND_CHUNKS + 1
    qkv = (h @ w_qkv).reshape(b, s, 3, N_HEADS, HEAD_DIM)
    q = qkv[:, :, 0].reshape(b, nc, CHUNK, N_HEADS, HEAD_DIM)
    pad = ((0, 0), (BAND_CHUNKS * CHUNK, 0), (0, 0), (0, 0))
    kp = jnp.pad(qkv[:, :, 1], pad).reshape(b, nc + BAND_CHUNKS, CHUNK, N_HEADS, HEAD_DIM)
    vp = jnp.pad(qkv[:, :, 2], pad).reshape(b, nc + BAND_CHUNKS, CHUNK, N_HEADS, HEAD_DIM)
    scale = HEAD_DIM ** -0.5
    scores = jnp.concatenate(
        [jnp.einsum('bcihd,bcjhd->bhcij', q, kp[:, o:o + nc]) for o in range(nband)],
        axis=-1).astype(jnp.float32) * scale
    i_pos = jnp.arange(CHUNK)[:, None]
    j_pos = jnp.arange(nband * CHUNK)[None, :]
    dist = i_pos + BAND_CHUNKS * CHUNK - j_pos
    idx = jnp.clip(dist, -REL_CLIP, REL_CLIP) + REL_CLIP
    bias = rel_bias[:, idx].astype(jnp.float32)
    key_chunk = jnp.arange(nc)[:, None] - BAND_CHUNKS + j_pos // CHUNK
    valid = (key_chunk >= 0)[:, None, :]
    scores = jnp.where(valid, scores + bias[:, None], -jnp.inf)
    p = jax.nn.softmax(scores, axis=-1).astype(vp.dtype)
    out = jnp.einsum('bhcij,bcjhd->bcihd', p[..., :CHUNK], vp[:, 0:nc])
    for o in range(1, nband):
        out = out + jnp.einsum('bhcij,bcjhd->bcihd', p[..., o * CHUNK:(o + 1) * CHUNK], vp[:, o:o + nc])
    return out.reshape(b, s, D_MODEL) @ w_o


def swiglu_ffn(h, w_in, w_out):
    g, u = jnp.split(h @ w_in, 2, axis=-1)
    return (jax.nn.silu(g) * u) @ w_out


def _n_layers_of(m):
    return len(range(m, DEPTH, N_MIXERS))


def _dense(key, shape, fan_in, gain=1.0):
    return jax.random.normal(key, shape, jnp.float32) * (gain * fan_in ** -0.5)


def setup_inputs(seed: int = 0) -> dict:
    key = jax.random.key(seed)
    ks = jax.random.split(key, 21)
    na, nb, ncv, nd = (_n_layers_of(m) for m in range(N_MIXERS))
    D = D_MODEL
    nrm = lambda k, shape: jax.random.normal(k, shape, jnp.float32)
    return {
        "x": nrm(ks[0], (BATCH, SEQ, D)),
        "c": nrm(ks[1], (BATCH, D)),
        "ada_w": _dense(ks[2], (DEPTH, D, 6 * D), D, 0.5),
        "ada_b": 0.02 * nrm(ks[3], (DEPTH, 6 * D)),
        "norm_g": 1.0 + 0.1 * nrm(ks[4], (DEPTH, 4, D)),
        "ffn_w_in": _dense(ks[5], (DEPTH, D, 2 * D_FF), D),
        "ffn_w_out": _dense(ks[6], (DEPTH, D_FF, D), D_FF),
        "sb_w_qkv": _dense(ks[7], (na, D, 3 * D), D),
        "sb_w_o": _dense(ks[8], (na, D, D), D),
        "sg_w_in": _dense(ks[9], (nb, D, 2 * GMLP_WIDTH), D),
        "sg_ln_g": 1.0 + 0.1 * nrm(ks[10], (nb, GMLP_WIDTH)),
        "sg_w_s": _dense(ks[11], (nb, GMLP_GROUPS, GMLP_BLOCK, GMLP_BLOCK), GMLP_BLOCK),
        "sg_bias": 1.0 + 0.1 * nrm(ks[12], (nb, GMLP_GROUPS, GMLP_BLOCK)),
        "sg_w_out": _dense(ks[13], (nb, GMLP_WIDTH, D), GMLP_WIDTH),
        "sc_w_in": _dense(ks[14], (ncv, D, 3 * D), D),
        "sc_conv_w": _dense(ks[15], (ncv, CONV_WIDTH, D), CONV_WIDTH),
        "sc_w_out": _dense(ks[16], (ncv, D, D), D),
        "cb_w_qkv": _dense(ks[17], (nd, D, 3 * D), D),
        "cb_rel_bias": 0.5 * nrm(ks[18], (nd, N_HEADS, 2 * REL_CLIP + 1)),
        "cb_w_o": _dense(ks[19], (nd, D, D), D),
    }


def reference(x, c, ada_w, ada_b, norm_g, ffn_w_in, ffn_w_out,
              sb_w_qkv, sb_w_o,
              sg_w_in, sg_ln_g, sg_w_s, sg_bias, sg_w_out,
              sc_w_in, sc_conv_w, sc_w_out,
              cb_w_qkv, cb_rel_bias, cb_w_o):
    mod_all = jnp.einsum('bd,lde->lbe', jax.nn.silu(c), ada_w) + ada_b[:, None]
    for i in range(DEPTH):
        m, r = i % N_MIXERS, i // N_MIXERS
        sh_m, sc_m, gt_m, sh_f, sc_f, gt_f = jnp.split(mod_all[i][:, None, :], 6, axis=-1)
        h = modulate(rms_norm(x, norm_g[i, 0]), sh_m, sc_m)
        if m == 0:
            y = stick_breaking_mixer(h, sb_w_qkv[r], sb_w_o[r])
        elif m == 1:
            y = spatial_gating_mixer(h, sg_w_in[r], sg_ln_g[r], sg_w_s[r], sg_bias[r], sg_w_out[r])
        elif m == 2:
            y = short_conv_mixer(h, sc_w_in[r], sc_conv_w[r], sc_w_out[r])
        else:
            y = chunk_band_attention_mixer(h, cb_w_qkv[r], cb_rel_bias[r], cb_w_o[r])
        x = x + gt_m * rms_norm(y, norm_g[i, 1])
        h = modulate(rms_norm(x, norm_g[i, 2]), sh_f, sc_f)
        y = swiglu_ffn(h, ffn_w_in[i], ffn_w_out[i])
        x = x + gt_f * rms_norm(y, norm_g[i, 3])
    return x
```

```python
import functools

import jax
import jax.numpy as jnp
from jax import lax
from jax.experimental import pallas as pl
from jax.experimental.pallas import tpu as pltpu

D_MODEL = 1024
DEPTH = 4
N_MIXERS = 4
N_HEADS = 16
HEAD_DIM = D_MODEL // N_HEADS
CHUNK = 64
GMLP_BLOCK = 128
GMLP_WIDTH = 2 * D_MODEL
GMLP_GROUPS = 8
GMLP_GROUP_CH = GMLP_WIDTH // GMLP_GROUPS
CONV_WIDTH = 3
BAND_CHUNKS = 8
REL_CLIP = 128
D_FF = 2816
EPS = 1e-6

LANES = 128
HEADS_PER_LANE_TILE = LANES // HEAD_DIM
N_HEAD_PAIRS = N_HEADS // HEADS_PER_LANE_TILE
VMEM_LIMIT = 56 * 1024 * 1024

ROW_TILE = 512
FFN_CHUNK = 256
SB_QBLK = 128
SB_KBLK = 256
BAND_QBLK = 2 * CHUNK
BAND_WIN = (BAND_CHUNKS + 2) * CHUNK
BAND_TBL = BAND_WIN + BAND_CHUNKS * CHUNK
NEG_BIG = -1e30

BF16 = jnp.bfloat16
F32 = jnp.float32


def _cparams(*sem):
    return pltpu.CompilerParams(dimension_semantics=sem, vmem_limit_bytes=VMEM_LIMIT)


def _resident(shape):
    zeros = (0,) * len(shape)
    return pl.BlockSpec(shape, lambda *_: zeros, pipeline_mode=pl.Buffered(1))


def _rms_scale(v):
    return lax.rsqrt(jnp.mean(v * v, axis=-1, keepdims=True) + EPS)


def _prologue(x, g, shift, scale):
    y = x * _rms_scale(x) * g
    return (y * (1.0 + scale) + shift).astype(BF16)


def _epilogue(x, y, g, gate):
    return x + gate * (y * _rms_scale(y) * g)


def _ada_kernel(c_ref, w_ref, b_ref, o_ref):
    c = c_ref[...]
    a = (c * jax.nn.sigmoid(c)).astype(BF16)
    w = w_ref[0].astype(BF16)
    o_ref[0] = jnp.dot(a, w, preferred_element_type=F32) + b_ref[0]


def _ada_modulation(c, ada_w, ada_b):
    b = c.shape[0]
    rows = 8
    n = ada_w.shape[-1]
    tn = 1536
    c_pad = jnp.zeros((rows, D_MODEL), F32).at[:b].set(c)
    out = pl.pallas_call(
        _ada_kernel,
        grid=(DEPTH, n // tn),
        in_specs=[
            pl.BlockSpec((rows, D_MODEL), lambda l, j: (0, 0)),
            pl.BlockSpec((1, D_MODEL, tn), lambda l, j: (l, 0, j)),
            pl.BlockSpec((1, 1, tn), lambda l, j: (l, 0, j)),
        ],
        out_specs=pl.BlockSpec((1, rows, tn), lambda l, j: (l, 0, j)),
        out_shape=jax.ShapeDtypeStruct((DEPTH, rows, n), F32),
        compiler_params=_cparams("parallel", "parallel"),
        name="ada_modulation",
    )(c_pad, ada_w, ada_b.reshape(DEPTH, 1, n))
    return out[:, :b].reshape(DEPTH, b, 6, D_MODEL)


def _proj_kernel(x_ref, mod_ref, g_ref, w_ref, o_ref, *, mod_row, n_chunk):
    h = _prologue(x_ref[0], g_ref[...], mod_ref[0, mod_row:mod_row + 1, :],
                  mod_ref[0, mod_row + 1:mod_row + 2, :])
    n = w_ref.shape[1]
    for c in range(n // n_chunk):
        sl = slice(c * n_chunk, (c + 1) * n_chunk)
        o_ref[0, :, sl] = jnp.dot(h, w_ref[:, sl], preferred_element_type=F32).astype(BF16)


def _norm_mod_proj(x, mod, g, w, mod_row):
    b, s, d = x.shape
    n = w.shape[1]
    return pl.pallas_call(
        functools.partial(_proj_kernel, mod_row=mod_row, n_chunk=512),
        grid=(b, s // ROW_TILE),
        in_specs=[
            pl.BlockSpec((1, ROW_TILE, d), lambda bi, i: (bi, i, 0)),
            pl.BlockSpec((1, 6, d), lambda bi, i: (bi, 0, 0)),
            _resident((1, d)),
            _resident((d, n)),
        ],
        out_specs=pl.BlockSpec((1, ROW_TILE, n), lambda bi, i: (bi, i, 0)),
        out_shape=jax.ShapeDtypeStruct((b, s, n), BF16),
        compiler_params=_cparams("parallel", "parallel"),
        name="norm_mod_proj",
    )(x, mod, g.reshape(1, d), w)


def _out_kernel(a_ref, x_ref, mod_ref, g_ref, w_ref, o_ref, *, gate_row):
    y = jnp.dot(a_ref[0], w_ref[...], preferred_element_type=F32)
    o_ref[0] = _epilogue(x_ref[0], y, g_ref[...], mod_ref[0, gate_row:gate_row + 1, :])


def _proj_norm_residual(a, x, mod, g, w, gate_row):
    b, s, d = x.shape
    k = a.shape[-1]
    return pl.pallas_call(
        functools.partial(_out_kernel, gate_row=gate_row),
        grid=(b, s // ROW_TILE),
        in_specs=[
            pl.BlockSpec((1, ROW_TILE, k), lambda bi, i: (bi, i, 0)),
            pl.BlockSpec((1, ROW_TILE, d), lambda bi, i: (bi, i, 0)),
            pl.BlockSpec((1, 6, d), lambda bi, i: (bi, 0, 0)),
            _resident((1, d)),
            _resident((k, d)),
        ],
        out_specs=pl.BlockSpec((1, ROW_TILE, d), lambda bi, i: (bi, i, 0)),
        out_shape=jax.ShapeDtypeStruct((b, s, d), F32),
        compiler_params=_cparams("parallel", "parallel"),
        name="proj_norm_residual",
    )(a, x, mod, g.reshape(1, d), w)


def _ffn_kernel(x_ref, mod_ref, g_ref, win_ref, wout_ref, o_ref, acc_ref):
    x = x_ref[0]
    h = _prologue(x, g_ref[0:1, :], mod_ref[0, 3:4, :], mod_ref[0, 4:5, :])
    for c in range(D_FF // FFN_CHUNK):
        gu = jnp.dot(h, win_ref[c], preferred_element_type=F32)
        gate, up = gu[:, :FFN_CHUNK], gu[:, FFN_CHUNK:]
        a = (gate * jax.nn.sigmoid(gate) * up).astype(BF16)
        part = jnp.dot(a, wout_ref[c], preferred_element_type=F32)
        if c == 0:
            acc_ref[...] = part
        else:
            acc_ref[...] += part
    o_ref[0] = _epilogue(x, acc_ref[...], g_ref[1:2, :], mod_ref[0, 5:6, :])


def _ffn_sublayer(x, mod, g2, w_in_r, w_out_r):
    b, s, d = x.shape
    nch = D_FF // FFN_CHUNK
    return pl.pallas_call(
        _ffn_kernel,
        grid=(b, s // ROW_TILE),
        in_specs=[
            pl.BlockSpec((1, ROW_TILE, d), lambda bi, i: (bi, i, 0)),
            pl.BlockSpec((1, 6, d), lambda bi, i: (bi, 0, 0)),
            _resident((2, d)),
            _resident((nch, d, 2 * FFN_CHUNK)),
            _resident((nch, FFN_CHUNK, d)),
        ],
        out_specs=pl.BlockSpec((1, ROW_TILE, d), lambda bi, i: (bi, i, 0)),
        out_shape=jax.ShapeDtypeStruct((b, s, d), F32),
        scratch_shapes=[pltpu.VMEM((ROW_TILE, d), F32)],
        compiler_params=_cparams("parallel", "parallel"),
        name="ffn_sublayer",
    )(x, mod, g2, w_in_r, w_out_r)


def _ffn_weights(w_in, w_out):
    nch = D_FF // FFN_CHUNK
    gate = w_in[:, :D_FF].reshape(D_MODEL, nch, FFN_CHUNK)
    up = w_in[:, D_FF:].reshape(D_MODEL, nch, FFN_CHUNK)
    w_in_r = jnp.concatenate([gate, up], axis=-1).transpose(1, 0, 2).astype(BF16)
    return w_in_r, w_out.reshape(nch, FFN_CHUNK, D_MODEL).astype(BF16)


def _gelu_tanh(x):
    c = 0.7978845608028654
    return 0.5 * x * (1.0 + jnp.tanh(c * (x + 0.044715 * (x * x * x))))


def _sgu_kernel(x_ref, mod_ref, g_ref, win_ref, lng_ref, ws_ref, sb_ref, wout_ref,
                o_ref, v_ref, y_ref):
    x = x_ref[0]
    h = _prologue(x, g_ref[0:1, :], mod_ref[0, 0:1, :], mod_ref[0, 1:2, :])
    n_chunk = 512
    half = GMLP_WIDTH // n_chunk
    for c in range(half):
        sl = slice(GMLP_WIDTH + c * n_chunk, GMLP_WIDTH + (c + 1) * n_chunk)
        v_ref[:, c * n_chunk:(c + 1) * n_chunk] = _gelu_tanh(
            jnp.dot(h, win_ref[:, sl], preferred_element_type=F32))
    v = v_ref[...]
    mu = jnp.mean(v, axis=-1, keepdims=True)
    vc = v - mu
    var = jnp.mean(vc * vc, axis=-1, keepdims=True)
    vn = (vc * lax.rsqrt(var + EPS) * lng_ref[...]).astype(BF16)
    r = lax.broadcasted_iota(jnp.int32, (GMLP_BLOCK, GMLP_BLOCK), 0) // CHUNK
    cidx = lax.broadcasted_iota(jnp.int32, (GMLP_BLOCK, GMLP_BLOCK), 1) // CHUNK
    keep = r >= cidx
    for gi in range(GMLP_GROUPS):
        cs = slice(gi * GMLP_GROUP_CH, (gi + 1) * GMLP_GROUP_CH)
        u = _gelu_tanh(jnp.dot(h, win_ref[:, cs], preferred_element_type=F32))
        ws = jnp.where(keep, ws_ref[gi], 0.0).astype(BF16)
        bias = sb_ref[:, gi:gi + 1]
        for blk in range(ROW_TILE // GMLP_BLOCK):
            rs = slice(blk * GMLP_BLOCK, (blk + 1) * GMLP_BLOCK)
            sv = jnp.dot(ws, vn[rs, cs], preferred_element_type=F32) + bias
            y_ref[rs, cs] = (u[rs] * sv).astype(BF16)
    y = jnp.dot(y_ref[...], wout_ref[...], preferred_element_type=F32)
    o_ref[0] = _epilogue(x, y, g_ref[1:2, :], mod_ref[0, 2:3, :])


def _sgu_sublayer(x, mod, g2, w_in, ln_g, w_s, s_bias, w_out):
    b, s, d = x.shape
    return pl.pallas_call(
        _sgu_kernel,
        grid=(b, s // ROW_TILE),
        in_specs=[
            pl.BlockSpec((1, ROW_TILE, d), lambda bi, i: (bi, i, 0)),
            pl.BlockSpec((1, 6, d), lambda bi, i: (bi, 0, 0)),
            _resident((2, d)),
            _resident((d, 2 * GMLP_WIDTH)),
            _resident((1, GMLP_WIDTH)),
            _resident((GMLP_GROUPS, GMLP_BLOCK, GMLP_BLOCK)),
            _resident((GMLP_BLOCK, GMLP_GROUPS)),
            _resident((GMLP_WIDTH, d)),
        ],
        out_specs=pl.BlockSpec((1, ROW_TILE, d), lambda bi, i: (bi, i, 0)),
        out_shape=jax.ShapeDtypeStruct((b, s, d), F32),
        scratch_shapes=[pltpu.VMEM((ROW_TILE, GMLP_WIDTH), F32),
                        pltpu.VMEM((ROW_TILE, GMLP_WIDTH), BF16)],
        compiler_params=_cparams("parallel", "parallel"),
        name="sgu_sublayer",
    )(x, mod, g2, w_in.astype(BF16), ln_g.reshape(1, GMLP_WIDTH), w_s, s_bias.T,
      w_out.astype(BF16))


CONV_HALO = 8


def _conv_kernel(x_ref, mod_ref, g_ref, win_ref, cw_ref, wout_ref, o_ref, ext_ref, t_ref):
    i = pl.program_id(1)
    x = x_ref[0]
    h = _prologue(x, g_ref[0:1, :], mod_ref[0, 0:1, :], mod_ref[0, 1:2, :])

    @pl.when(i == 0)
    def _():
        ext_ref[0:CONV_HALO, :] = jnp.zeros((CONV_HALO, D_MODEL), F32)

    @pl.when(i != 0)
    def _():
        ext_ref[0:CONV_HALO, :] = ext_ref[ROW_TILE:ROW_TILE + CONV_HALO, :]

    n_chunk = 512
    for c in range(D_MODEL // n_chunk):
        cs = slice(c * n_chunk, (c + 1) * n_chunk)
        gate_c = jnp.dot(h, win_ref[:, D_MODEL + c * n_chunk:D_MODEL + (c + 1) * n_chunk],
                         preferred_element_type=F32)
        xt = jnp.dot(h, win_ref[:, 2 * D_MODEL + c * n_chunk:2 * D_MODEL + (c + 1) * n_chunk],
                     preferred_element_type=F32)
        ext_ref[CONV_HALO:, cs] = gate_c * xt
    for c in range(D_MODEL // n_chunk):
        cs = slice(c * n_chunk, (c + 1) * n_chunk)
        gate_b = jnp.dot(h, win_ref[:, cs], preferred_element_type=F32)
        yc = (cw_ref[0:1, cs] * ext_ref[CONV_HALO - 2:CONV_HALO - 2 + ROW_TILE, cs]
              + cw_ref[1:2, cs] * ext_ref[CONV_HALO - 1:CONV_HALO - 1 + ROW_TILE, cs]
              + cw_ref[2:3, cs] * ext_ref[CONV_HALO:, cs])
        t_ref[:, cs] = (gate_b * yc).astype(BF16)
    y = jnp.dot(t_ref[...], wout_ref[...], preferred_element_type=F32)
    o_ref[0] = _epilogue(x, y, g_ref[1:2, :], mod_ref[0, 2:3, :])


def _conv_sublayer(x, mod, g2, w_in, conv_w, w_out):
    b, s, d = x.shape
    return pl.pallas_call(
        _conv_kernel,
        grid=(b, s // ROW_TILE),
        in_specs=[
            pl.BlockSpec((1, ROW_TILE, d), lambda bi, i: (bi, i, 0)),
            pl.BlockSpec((1, 6, d), lambda bi, i: (bi, 0, 0)),
            _resident((2, d)),
            _resident((d, 3 * d)),
            _resident((CONV_WIDTH, d)),
            _resident((d, d)),
        ],
        out_specs=pl.BlockSpec((1, ROW_TILE, d), lambda bi, i: (bi, i, 0)),
        out_shape=jax.ShapeDtypeStruct((b, s, d), F32),
        scratch_shapes=[pltpu.VMEM((ROW_TILE + CONV_HALO, d), F32),
                        pltpu.VMEM((ROW_TILE, d), BF16)],
        compiler_params=_cparams("parallel", "arbitrary"),
        name="conv_sublayer",
    )(x, mod, g2, w_in.astype(BF16), conv_w, w_out.astype(BF16))


def _softplus(z):
    return jnp.maximum(z, 0.0) + jnp.log(1.0 + jnp.exp(-jnp.abs(z)))


def _sb_kernel(q_ref, k_ref, v_ref, o_ref):
    s = q_ref.shape[1]
    lane = lax.broadcasted_iota(jnp.int32, (1, LANES), 1)
    head_masks = [(lane // HEAD_DIM == hh) for hh in range(HEADS_PER_LANE_TILE)]
    tri = jnp.where(lax.broadcasted_iota(jnp.int32, (SB_KBLK, SB_KBLK), 0)
                    >= lax.broadcasted_iota(jnp.int32, (SB_KBLK, SB_KBLK), 1),
                    1.0, 0.0).astype(BF16)
    row = lax.broadcasted_iota(jnp.int32, (SB_QBLK, SB_KBLK), 0)
    col = lax.broadcasted_iota(jnp.int32, (SB_QBLK, SB_KBLK), 1)
    scale = HEAD_DIM ** -0.5

    def q_block(i, _):
        q0 = pl.multiple_of(i * SB_QBLK, SB_QBLK)
        q = q_ref[0, pl.ds(q0, SB_QBLK), :] * scale
        zero = jnp.zeros_like(q)
        qh = [jnp.where(m, q, zero) for m in head_masks]
        jd = q0 // SB_KBLK

        def key_block(j, carry, diag):
            k0 = pl.multiple_of(j * SB_KBLK, SB_KBLK)
            kb = k_ref[0, pl.ds(k0, SB_KBLK), :]
            vb = v_ref[0, pl.ds(k0, SB_KBLK), :]
            if diag:
                causal = (col + k0) < (row + q0)
            new = []
            for hh in range(HEADS_PER_LANE_TILE):
                acc, run = carry[hh]
                z = lax.dot_general(qh[hh], kb, (((1,), (1,)), ((), ())),
                                    preferred_element_type=F32)
                sp = _softplus(z)
                if diag:
                    sp = jnp.where(causal, sp, 0.0)
                rest = jnp.dot(sp.astype(BF16), tri, preferred_element_type=F32)
                a = jnp.exp(z - rest - run)
                if diag:
                    a = jnp.where(causal, a, 0.0)
                acc = acc + jnp.dot(a.astype(BF16), vb, preferred_element_type=F32)
                run = run + jnp.sum(sp, axis=-1, keepdims=True)
                new.append((acc, run))
            return tuple(new)

        init = tuple((jnp.zeros((SB_QBLK, LANES), F32), jnp.zeros((SB_QBLK, 1), F32))
                     for _ in range(HEADS_PER_LANE_TILE))
        carry = key_block(jd, init, True)
        carry = lax.fori_loop(0, jd, lambda t, cr: key_block(jd - 1 - t, cr, False), carry)
        out = carry[0][0]
        for hh in range(1, HEADS_PER_LANE_TILE):
            out = jnp.where(head_masks[hh], carry[hh][0], out)
        o_ref[0, pl.ds(q0, SB_QBLK), :] = out.astype(BF16)
        return 0

    lax.fori_loop(0, s // SB_QBLK, q_block, 0)


def _stick_breaking_attention(qkv):
    b, s, _ = qkv.shape
    np_ = N_HEAD_PAIRS
    return pl.pallas_call(
        _sb_kernel,
        grid=(b, np_),
        in_specs=[
            pl.BlockSpec((1, s, LANES), lambda bi, p: (bi, 0, p)),
            pl.BlockSpec((1, s, LANES), lambda bi, p: (bi, 0, np_ + p)),
            pl.BlockSpec((1, s, LANES), lambda bi, p: (bi, 0, 2 * np_ + p)),
        ],
        out_specs=pl.BlockSpec((1, s, LANES), lambda bi, p: (bi, 0, p)),
        out_shape=jax.ShapeDtypeStruct((b, s, D_MODEL), BF16),
        compiler_params=_cparams("parallel", "parallel"),
        name="stick_breaking_attention",
    )(qkv, qkv, qkv)


def _band_kernel(q_ref, k_ref, v_ref, tbl_ref, o_ref):
    s = q_ref.shape[1]
    lead = BAND_CHUNKS * CHUNK
    lane = lax.broadcasted_iota(jnp.int32, (1, LANES), 1)
    head_masks = [(lane // HEAD_DIM == hh) for hh in range(HEADS_PER_LANE_TILE)]
    row_chunk = lax.broadcasted_iota(jnp.int32, (BAND_QBLK, BAND_WIN), 0) // CHUNK
    col_chunk = lax.broadcasted_iota(jnp.int32, (BAND_QBLK, BAND_WIN), 1) // CHUNK
    scale = HEAD_DIM ** -0.5

    def q_block(i, first_chunk, tbl_off):
        q0, k0 = i * BAND_QBLK, first_chunk * CHUNK
        if not isinstance(i, int):
            q0, k0 = pl.multiple_of(q0, BAND_QBLK), pl.multiple_of(k0, BAND_QBLK)
        q = q_ref[0, pl.ds(q0, BAND_QBLK), :] * scale
        zero = jnp.zeros_like(q)
        kb = k_ref[0, pl.ds(k0, BAND_WIN), :]
        vb = v_ref[0, pl.ds(k0, BAND_WIN), :]
        q_chunk = row_chunk + i * (BAND_QBLK // CHUNK)
        k_chunk = col_chunk + first_chunk
        valid = (k_chunk <= q_chunk) & (k_chunk >= q_chunk - BAND_CHUNKS)
        out = None
        for hh in range(HEADS_PER_LANE_TILE):
            qh = jnp.where(head_masks[hh], q, zero)
            sc = lax.dot_general(qh, kb, (((1,), (1,)), ((), ())), preferred_element_type=F32)
            sc = sc + tbl_ref[hh, :, tbl_off:tbl_off + BAND_WIN]
            sc = jnp.where(valid, sc, NEG_BIG)
            m = jnp.max(sc, axis=-1, keepdims=True)
            e = jnp.exp(sc - m)
            denom = jnp.sum(e, axis=-1, keepdims=True)
            oh = jnp.dot(e.astype(BF16), vb, preferred_element_type=F32) / denom
            out = oh if out is None else jnp.where(head_masks[hh], oh, out)
        o_ref[0, pl.ds(q0, BAND_QBLK), :] = out.astype(BF16)

    n_head = lead // BAND_QBLK
    for i in range(n_head):
        q_block(i, 0, lead - i * BAND_QBLK)

    def body(i, _):
        q_block(i, i * (BAND_QBLK // CHUNK) - BAND_CHUNKS, 0)
        return 0

    lax.fori_loop(n_head, s // BAND_QBLK, body, 0)


def _band_bias_table(rel_bias):
    lead = BAND_CHUNKS * CHUNK
    r = jnp.arange(BAND_QBLK)[:, None]
    u = jnp.arange(BAND_TBL)[None, :]
    idx = jnp.clip(r + lead - u, -REL_CLIP, REL_CLIP) + REL_CLIP
    return rel_bias[:, idx].astype(F32)


def _band_attention(qkv, rel_bias):
    b, s, _ = qkv.shape
    np_ = N_HEAD_PAIRS
    tbl = _band_bias_table(rel_bias)
    return pl.pallas_call(
        _band_kernel,
        grid=(b, np_),
        in_specs=[
            pl.BlockSpec((1, s, LANES), lambda bi, p: (bi, 0, p)),
            pl.BlockSpec((1, s, LANES), lambda bi, p: (bi, 0, np_ + p)),
            pl.BlockSpec((1, s, LANES), lambda bi, p: (bi, 0, 2 * np_ + p)),
            pl.BlockSpec((HEADS_PER_LANE_TILE, BAND_QBLK, BAND_TBL), lambda bi, p: (p, 0, 0)),
        ],
        out_specs=pl.BlockSpec((1, s, LANES), lambda bi, p: (bi, 0, p)),
        out_shape=jax.ShapeDtypeStruct((b, s, D_MODEL), BF16),
        compiler_params=_cparams("parallel", "parallel"),
        name="band_attention",
    )(qkv, qkv, qkv, tbl)


def kernel(x, c, ada_w, ada_b, norm_g, ffn_w_in, ffn_w_out, sb_w_qkv, sb_w_o, sg_w_in, sg_ln_g, sg_w_s, sg_bias, sg_w_out, sc_w_in, sc_conv_w, sc_w_out, cb_w_qkv, cb_rel_bias, cb_w_o):
    assert x.shape[1] % ROW_TILE == 0 and x.shape[2] == D_MODEL
    mod_all = _ada_modulation(c, ada_w, ada_b)
    for i in range(DEPTH):
        m, r = i % N_MIXERS, i // N_MIXERS
        mod = mod_all[i]
        g_mix = norm_g[i, 0:2]
        if m == 0:
            qkv = _norm_mod_proj(x, mod, norm_g[i, 0], sb_w_qkv[r].astype(BF16), 0)
            o = _stick_breaking_attention(qkv)
            x = _proj_norm_residual(o, x, mod, norm_g[i, 1], sb_w_o[r].astype(BF16), 2)
        elif m == 1:
            x = _sgu_sublayer(x, mod, g_mix, sg_w_in[r], sg_ln_g[r], sg_w_s[r], sg_bias[r],
                              sg_w_out[r])
        elif m == 2:
            x = _conv_sublayer(x, mod, g_mix, sc_w_in[r], sc_conv_w[r], sc_w_out[r])
        else:
            qkv = _norm_mod_proj(x, mod, norm_g[i, 0], cb_w_qkv[r].astype(BF16), 0)
            o = _band_attention(qkv, cb_rel_bias[r])
            x = _proj_norm_residual(o, x, mod, norm_g[i, 1], cb_w_o[r].astype(BF16), 2)
        w_in_r, w_out_r = _ffn_weights(ffn_w_in[i], ffn_w_out[i])
        x = _ffn_sublayer(x, mod, norm_g[i, 2:4], w_in_r, w_out_r)
    return x
```

```python
import functools

import jax
import jax.numpy as jnp
from jax import lax
from jax.experimental import pallas as pl
from jax.experimental.pallas import tpu as pltpu

D_MODEL = 1024
DEPTH = 4
N_MIXERS = 4
N_HEADS = 16
HEAD_DIM = D_MODEL // N_HEADS
CHUNK = 64
GMLP_BLOCK = 128
GMLP_WIDTH = 2 * D_MODEL
GMLP_GROUPS = 8
GMLP_GROUP_CH = GMLP_WIDTH // GMLP_GROUPS
CONV_WIDTH = 3
BAND_CHUNKS = 8
REL_CLIP = 128
D_FF = 2816
EPS = 1e-6

LANES = 128
HEADS_PER_LANE_TILE = LANES // HEAD_DIM
N_HEAD_PAIRS = N_HEADS // HEADS_PER_LANE_TILE
VMEM_LIMIT = 56 * 1024 * 1024

ROW_TILE = 512
FFN_CHUNK = 256
SB_QBLK = 128
SB_KBLK = 256
SB_QSUP = 1024
SB_CHAINS = SB_QSUP // SB_QBLK
BAND_QBLK = 2 * CHUNK
BAND_GROUP = 4
BAND_WIN = (BAND_CHUNKS + 2) * CHUNK
BAND_TBL = BAND_WIN + BAND_CHUNKS * CHUNK
NEG_BIG = -1e30

BF16 = jnp.bfloat16
F32 = jnp.float32


def _cparams(*sem):
    return pltpu.CompilerParams(dimension_semantics=sem, vmem_limit_bytes=VMEM_LIMIT)


def _resident(shape):
    zeros = (0,) * len(shape)
    return pl.BlockSpec(shape, lambda *_: zeros, pipeline_mode=pl.Buffered(1))


def _rms_scale(v):
    return lax.rsqrt(jnp.mean(v * v, axis=-1, keepdims=True) + EPS)


def _prologue(x, g, shift, scale):
    y = x * _rms_scale(x) * g
    return (y * (1.0 + scale) + shift).astype(BF16)


def _epilogue(x, y, g, gate):
    return x + gate * (y * _rms_scale(y) * g)


def _ada_kernel(c_ref, w_ref, b_ref, o_ref):
    c = c_ref[...]
    a = (c * jax.nn.sigmoid(c)).astype(BF16)
    w = w_ref[0].astype(BF16)
    o_ref[0] = jnp.dot(a, w, preferred_element_type=F32) + b_ref[0]


def _ada_modulation(c, ada_w, ada_b):
    b = c.shape[0]
    rows = 8
    n = ada_w.shape[-1]
    tn = 1536
    c_pad = jnp.zeros((rows, D_MODEL), F32).at[:b].set(c)
    out = pl.pallas_call(
        _ada_kernel,
        grid=(DEPTH, n // tn),
        in_specs=[
            pl.BlockSpec((rows, D_MODEL), lambda l, j: (0, 0)),
            pl.BlockSpec((1, D_MODEL, tn), lambda l, j: (l, 0, j)),
            pl.BlockSpec((1, 1, tn), lambda l, j: (l, 0, j)),
        ],
        out_specs=pl.BlockSpec((1, rows, tn), lambda l, j: (l, 0, j)),
        out_shape=jax.ShapeDtypeStruct((DEPTH, rows, n), F32),
        compiler_params=_cparams("parallel", "parallel"),
        name="ada_modulation",
    )(c_pad, ada_w, ada_b.reshape(DEPTH, 1, n))
    return out[:, :b].reshape(DEPTH, b, 6, D_MODEL)


def _proj_kernel(x_ref, mod_ref, g_ref, w_ref, o_ref, *, mod_row, n_chunk):
    h = _prologue(x_ref[0], g_ref[...], mod_ref[0, mod_row:mod_row + 1, :],
                  mod_ref[0, mod_row + 1:mod_row + 2, :])
    n = w_ref.shape[1]
    for c in range(n // n_chunk):
        sl = slice(c * n_chunk, (c + 1) * n_chunk)
        o_ref[0, :, sl] = jnp.dot(h, w_ref[:, sl], preferred_element_type=F32).astype(BF16)


def _norm_mod_proj(x, mod, g, w, mod_row):
    b, s, d = x.shape
    n = w.shape[1]
    return pl.pallas_call(
        functools.partial(_proj_kernel, mod_row=mod_row, n_chunk=512),
        grid=(b, s // ROW_TILE),
        in_specs=[
            pl.BlockSpec((1, ROW_TILE, d), lambda bi, i: (bi, i, 0)),
            pl.BlockSpec((1, 6, d), lambda bi, i: (bi, 0, 0)),
            _resident((1, d)),
            _resident((d, n)),
        ],
        out_specs=pl.BlockSpec((1, ROW_TILE, n), lambda bi, i: (bi, i, 0)),
        out_shape=jax.ShapeDtypeStruct((b, s, n), BF16),
        compiler_params=_cparams("parallel", "parallel"),
        name="norm_mod_proj",
    )(x, mod, g.reshape(1, d), w)


def _out_kernel(a_ref, x_ref, mod_ref, g_ref, w_ref, o_ref, *, gate_row):
    y = jnp.dot(a_ref[0], w_ref[...], preferred_element_type=F32)
    o_ref[0] = _epilogue(x_ref[0], y, g_ref[...], mod_ref[0, gate_row:gate_row + 1, :])


def _proj_norm_residual(a, x, mod, g, w, gate_row):
    b, s, d = x.shape
    k = a.shape[-1]
    return pl.pallas_call(
        functools.partial(_out_kernel, gate_row=gate_row),
        grid=(b, s // ROW_TILE),
        in_specs=[
            pl.BlockSpec((1, ROW_TILE, k), lambda bi, i: (bi, i, 0)),
            pl.BlockSpec((1, ROW_TILE, d), lambda bi, i: (bi, i, 0)),
            pl.BlockSpec((1, 6, d), lambda bi, i: (bi, 0, 0)),
            _resident((1, d)),
            _resident((k, d)),
        ],
        out_specs=pl.BlockSpec((1, ROW_TILE, d), lambda bi, i: (bi, i, 0)),
        out_shape=jax.ShapeDtypeStruct((b, s, d), F32),
        compiler_params=_cparams("parallel", "parallel"),
        name="proj_norm_residual",
    )(a, x, mod, g.reshape(1, d), w)


def _ffn_kernel(x_ref, mod_ref, g_ref, win_ref, wout_ref, o_ref, acc_ref):
    x = x_ref[0]
    h = _prologue(x, g_ref[0:1, :], mod_ref[0, 3:4, :], mod_ref[0, 4:5, :])
    for c in range(D_FF // FFN_CHUNK):
        gu = jnp.dot(h, win_ref[c], preferred_element_type=F32)
        gate, up = gu[:, :FFN_CHUNK], gu[:, FFN_CHUNK:]
        a = (gate * jax.nn.sigmoid(gate) * up).astype(BF16)
        part = jnp.dot(a, wout_ref[c], preferred_element_type=F32)
        if c == 0:
            acc_ref[...] = part
        else:
            acc_ref[...] += part
    o_ref[0] = _epilogue(x, acc_ref[...], g_ref[1:2, :], mod_ref[0, 5:6, :])


def _ffn_sublayer(x, mod, g2, w_in_r, w_out_r):
    b, s, d = x.shape
    nch = D_FF // FFN_CHUNK
    return pl.pallas_call(
        _ffn_kernel,
        grid=(b, s // ROW_TILE),
        in_specs=[
            pl.BlockSpec((1, ROW_TILE, d), lambda bi, i: (bi, i, 0)),
            pl.BlockSpec((1, 6, d), lambda bi, i: (bi, 0, 0)),
            _resident((2, d)),
            _resident((nch, d, 2 * FFN_CHUNK)),
            _resident((nch, FFN_CHUNK, d)),
        ],
        out_specs=pl.BlockSpec((1, ROW_TILE, d), lambda bi, i: (bi, i, 0)),
        out_shape=jax.ShapeDtypeStruct((b, s, d), F32),
        scratch_shapes=[pltpu.VMEM((ROW_TILE, d), F32)],
        compiler_params=_cparams("parallel", "parallel"),
        name="ffn_sublayer",
    )(x, mod, g2, w_in_r, w_out_r)


def _ffn_weights(w_in, w_out):
    nch = D_FF // FFN_CHUNK
    gate = w_in[:, :D_FF].reshape(D_MODEL, nch, FFN_CHUNK)
    up = w_in[:, D_FF:].reshape(D_MODEL, nch, FFN_CHUNK)
    w_in_r = jnp.concatenate([gate, up], axis=-1).transpose(1, 0, 2).astype(BF16)
    return w_in_r, w_out.reshape(nch, FFN_CHUNK, D_MODEL).astype(BF16)


def _gelu_tanh(x):
    c = 0.7978845608028654
    return 0.5 * x * (1.0 + jnp.tanh(c * (x + 0.044715 * (x * x * x))))


def _sgu_kernel(x_ref, mod_ref, g_ref, win_ref, lng_ref, ws_ref, sb_ref, wout_ref,
                o_ref, v_ref, y_ref):
    x = x_ref[0]
    h = _prologue(x, g_ref[0:1, :], mod_ref[0, 0:1, :], mod_ref[0, 1:2, :])
    n_chunk = 512
    half = GMLP_WIDTH // n_chunk
    for c in range(half):
        sl = slice(GMLP_WIDTH + c * n_chunk, GMLP_WIDTH + (c + 1) * n_chunk)
        v_ref[:, c * n_chunk:(c + 1) * n_chunk] = _gelu_tanh(
            jnp.dot(h, win_ref[:, sl], preferred_element_type=F32))
    v = v_ref[...]
    mu = jnp.mean(v, axis=-1, keepdims=True)
    vc = v - mu
    var = jnp.mean(vc * vc, axis=-1, keepdims=True)
    vn = (vc * lax.rsqrt(var + EPS) * lng_ref[...]).astype(BF16)
    r = lax.broadcasted_iota(jnp.int32, (GMLP_BLOCK, GMLP_BLOCK), 0) // CHUNK
    cidx = lax.broadcasted_iota(jnp.int32, (GMLP_BLOCK, GMLP_BLOCK), 1) // CHUNK
    keep = r >= cidx
    for gi in range(GMLP_GROUPS):
        cs = slice(gi * GMLP_GROUP_CH, (gi + 1) * GMLP_GROUP_CH)
        u = _gelu_tanh(jnp.dot(h, win_ref[:, cs], preferred_element_type=F32))
        ws = jnp.where(keep, ws_ref[gi], 0.0).astype(BF16)
        bias = sb_ref[:, gi:gi + 1]
        for blk in range(ROW_TILE // GMLP_BLOCK):
            rs = slice(blk * GMLP_BLOCK, (blk + 1) * GMLP_BLOCK)
            sv = jnp.dot(ws, vn[rs, cs], preferred_element_type=F32) + bias
            y_ref[rs, cs] = (u[rs] * sv).astype(BF16)
    y = jnp.dot(y_ref[...], wout_ref[...], preferred_element_type=F32)
    o_ref[0] = _epilogue(x, y, g_ref[1:2, :], mod_ref[0, 2:3, :])


def _sgu_sublayer(x, mod, g2, w_in, ln_g, w_s, s_bias, w_out):
    b, s, d = x.shape
    return pl.pallas_call(
        _sgu_kernel,
        grid=(b, s // ROW_TILE),
        in_specs=[
            pl.BlockSpec((1, ROW_TILE, d), lambda bi, i: (bi, i, 0)),
            pl.BlockSpec((1, 6, d), lambda bi, i: (bi, 0, 0)),
            _resident((2, d)),
            _resident((d, 2 * GMLP_WIDTH)),
            _resident((1, GMLP_WIDTH)),
            _resident((GMLP_GROUPS, GMLP_BLOCK, GMLP_BLOCK)),
            _resident((GMLP_BLOCK, GMLP_GROUPS)),
            _resident((GMLP_WIDTH, d)),
        ],
        out_specs=pl.BlockSpec((1, ROW_TILE, d), lambda bi, i: (bi, i, 0)),
        out_shape=jax.ShapeDtypeStruct((b, s, d), F32),
        scratch_shapes=[pltpu.VMEM((ROW_TILE, GMLP_WIDTH), F32),
                        pltpu.VMEM((ROW_TILE, GMLP_WIDTH), BF16)],
        compiler_params=_cparams("parallel", "parallel"),
        name="sgu_sublayer",
    )(x, mod, g2, w_in.astype(BF16), ln_g.reshape(1, GMLP_WIDTH), w_s, s_bias.T,
      w_out.astype(BF16))


CONV_HALO = 8


def _conv_kernel(x_ref, mod_ref, g_ref, win_ref, cw_ref, wout_ref, o_ref, ext_ref, t_ref):
    i = pl.program_id(1)
    x = x_ref[0]
    h = _prologue(x, g_ref[0:1, :], mod_ref[0, 0:1, :], mod_ref[0, 1:2, :])

    @pl.when(i == 0)
    def _():
        ext_ref[0:CONV_HALO, :] = jnp.zeros((CONV_HALO, D_MODEL), F32)

    @pl.when(i != 0)
    def _():
        ext_ref[0:CONV_HALO, :] = ext_ref[ROW_TILE:ROW_TILE + CONV_HALO, :]

    n_chunk = 512
    for c in range(D_MODEL // n_chunk):
        cs = slice(c * n_chunk, (c + 1) * n_chunk)
        gate_c = jnp.dot(h, win_ref[:, D_MODEL + c * n_chunk:D_MODEL + (c + 1) * n_chunk],
                         preferred_element_type=F32)
        xt = jnp.dot(h, win_ref[:, 2 * D_MODEL + c * n_chunk:2 * D_MODEL + (c + 1) * n_chunk],
                     preferred_element_type=F32)
        ext_ref[CONV_HALO:, cs] = gate_c * xt
    for c in range(D_MODEL // n_chunk):
        cs = slice(c * n_chunk, (c + 1) * n_chunk)
        gate_b = jnp.dot(h, win_ref[:, cs], preferred_element_type=F32)
        yc = (cw_ref[0:1, cs] * ext_ref[CONV_HALO - 2:CONV_HALO - 2 + ROW_TILE, cs]
              + cw_ref[1:2, cs] * ext_ref[CONV_HALO - 1:CONV_HALO - 1 + ROW_TILE, cs]
              + cw_ref[2:3, cs] * ext_ref[CONV_HALO:, cs])
        t_ref[:, cs] = (gate_b * yc).astype(BF16)
    y = jnp.dot(t_ref[...], wout_ref[...], preferred_element_type=F32)
    o_ref[0] = _epilogue(x, y, g_ref[1:2, :], mod_ref[0, 2:3, :])


def _conv_sublayer(x, mod, g2, w_in, conv_w, w_out):
    b, s, d = x.shape
    return pl.pallas_call(
        _conv_kernel,
        grid=(b, s // ROW_TILE),
        in_specs=[
            pl.BlockSpec((1, ROW_TILE, d), lambda bi, i: (bi, i, 0)),
            pl.BlockSpec((1, 6, d), lambda bi, i: (bi, 0, 0)),
            _resident((2, d)),
            _resident((d, 3 * d)),
            _resident((CONV_WIDTH, d)),
            _resident((d, d)),
        ],
        out_specs=pl.BlockSpec((1, ROW_TILE, d), lambda bi, i: (bi, i, 0)),
        out_shape=jax.ShapeDtypeStruct((b, s, d), F32),
        scratch_shapes=[pltpu.VMEM((ROW_TILE + CONV_HALO, d), F32),
                        pltpu.VMEM((ROW_TILE, d), BF16)],
        compiler_params=_cparams("parallel", "arbitrary"),
        name="conv_sublayer",
    )(x, mod, g2, w_in.astype(BF16), conv_w, w_out.astype(BF16))


def _sb_kernel(q_ref, k_ref, v_ref, o_ref, qm_ref, acc_ref, run_ref):
    s = q_ref.shape[1]
    nh = HEADS_PER_LANE_TILE
    lane = lax.broadcasted_iota(jnp.int32, (1, LANES), 1)
    head_masks = [(lane // HEAD_DIM == hh) for hh in range(nh)]
    tri = jnp.where(lax.broadcasted_iota(jnp.int32, (SB_KBLK, SB_KBLK), 0)
                    >= lax.broadcasted_iota(jnp.int32, (SB_KBLK, SB_KBLK), 1),
                    1.0, 0.0).astype(BF16)
    row = lax.broadcasted_iota(jnp.int32, (SB_QBLK, SB_KBLK), 0)
    col = lax.broadcasted_iota(jnp.int32, (SB_QBLK, SB_KBLK), 1)
    scale = HEAD_DIM ** -0.5

    rows_per_chunk = nh * SB_QBLK

    def block_step(k0, causal_of_chunk):
        c_lo = min(causal_of_chunk)
        r0 = c_lo * rows_per_chunk
        kb = k_ref[0, pl.ds(k0, SB_KBLK), :]
        vb = v_ref[0, pl.ds(k0, SB_KBLK), :]
        z = lax.dot_general(qm_ref[r0:, :], kb, (((1,), (1,)), ((), ())),
                             preferred_element_type=F32)
        chains = [(c, hh) for c in range(c_lo, SB_CHAINS) for hh in range(nh)]
        rows = [slice(n * SB_QBLK, (n + 1) * SB_QBLK) for n in range(len(chains))]
        sps = []
        for (c, _), rs in zip(chains, rows):
            zc = z[rs]
            sp = jnp.maximum(zc, 0.0) + jnp.log(1.0 + jnp.exp(-jnp.abs(zc)))
            if causal_of_chunk[c] is not None:
                sp = jnp.where(causal_of_chunk[c], sp, 0.0)
            sps.append(sp)
        sp_all = jnp.concatenate(sps, axis=0)
        rest = jnp.dot(sp_all.astype(BF16), tri, preferred_element_type=F32)
        run = run_ref[r0:, :]
        a_s = []
        for (c, _), rs in zip(chains, rows):
            run_c = jnp.concatenate([run[rs]] * (SB_KBLK // LANES), axis=1)
            a = jnp.exp(z[rs] - rest[rs] - run_c)
            if causal_of_chunk[c] is not None:
                a = jnp.where(causal_of_chunk[c], a, 0.0)
            a_s.append(a.astype(BF16))
        acc_ref[r0:, :] += jnp.dot(jnp.concatenate(a_s, axis=0), vb,
                                   preferred_element_type=F32)
        run_ref[r0:, :] = run + jnp.sum(sp_all, axis=-1, keepdims=True)

    def q_super(g, _):
        q0 = pl.multiple_of(g * SB_QSUP, SB_QSUP)
        for c in range(SB_CHAINS):
            q = q_ref[0, pl.ds(q0 + c * SB_QBLK, SB_QBLK), :] * scale
            for hh in range(nh):
                r0 = c * rows_per_chunk + hh * SB_QBLK
                qm_ref[r0:r0 + SB_QBLK, :] = jnp.where(head_masks[hh], q, jnp.zeros_like(q))
        acc_ref[...] = jnp.zeros_like(acc_ref)
        run_ref[...] = jnp.zeros_like(run_ref)
        for jj in reversed(range(SB_QSUP // SB_KBLK)):
            first_key = jj * SB_KBLK
            causal_of_chunk = {}
            for c in range(SB_CHAINS):
                first_row = c * SB_QBLK
                if first_key >= first_row + SB_QBLK - 1:
                    continue
                full = first_key + SB_KBLK - 1 < first_row
                causal_of_chunk[c] = None if full else (col + first_key) < (row + first_row)
            assert sorted(causal_of_chunk) == list(range(min(causal_of_chunk), SB_CHAINS))
            block_step(q0 + first_key, causal_of_chunk)

        all_full = {c: None for c in range(SB_CHAINS)}

        def key_block(t, _):
            block_step(pl.multiple_of(q0 - (t + 1) * SB_KBLK, SB_KBLK), all_full)
            return 0

        lax.fori_loop(0, g * (SB_QSUP // SB_KBLK), key_block, 0)
        for c in range(SB_CHAINS):
            r0 = c * rows_per_chunk
            out = acc_ref[r0:r0 + SB_QBLK, :]
            for hh in range(1, nh):
                out = jnp.where(head_masks[hh],
                                acc_ref[r0 + hh * SB_QBLK:r0 + (hh + 1) * SB_QBLK, :], out)
            o_ref[0, pl.ds(q0 + c * SB_QBLK, SB_QBLK), :] = out.astype(BF16)
        return 0

    lax.fori_loop(0, s // SB_QSUP, q_super, 0)


def _stick_breaking_attention(qkv):
    b, s, _ = qkv.shape
    np_ = N_HEAD_PAIRS
    return pl.pallas_call(
        _sb_kernel,
        grid=(b, np_),
        in_specs=[
            pl.BlockSpec((1, s, LANES), lambda bi, p: (bi, 0, p)),
            pl.BlockSpec((1, s, LANES), lambda bi, p: (bi, 0, np_ + p)),
            pl.BlockSpec((1, s, LANES), lambda bi, p: (bi, 0, 2 * np_ + p)),
        ],
        out_specs=pl.BlockSpec((1, s, LANES), lambda bi, p: (bi, 0, p)),
        out_shape=jax.ShapeDtypeStruct((b, s, D_MODEL), BF16),
        scratch_shapes=[
            pltpu.VMEM((SB_QSUP * HEADS_PER_LANE_TILE, LANES), BF16),
            pltpu.VMEM((SB_QSUP * HEADS_PER_LANE_TILE, LANES), F32),
            pltpu.VMEM((SB_QSUP * HEADS_PER_LANE_TILE, LANES), F32),
        ],
        compiler_params=_cparams("parallel", "parallel"),
        name="stick_breaking_attention",
    )(qkv, qkv, qkv)


def _band_kernel(q_ref, k_ref, v_ref, tbl_ref, o_ref, bias_ref):
    s = q_ref.shape[1]
    lead = BAND_CHUNKS * CHUNK
    lane = lax.broadcasted_iota(jnp.int32, (1, LANES), 1)
    head_masks = [(lane // HEAD_DIM == hh) for hh in range(HEADS_PER_LANE_TILE)]
    row_chunk = lax.broadcasted_iota(jnp.int32, (BAND_QBLK, BAND_WIN), 0) // CHUNK
    col_chunk = lax.broadcasted_iota(jnp.int32, (BAND_QBLK, BAND_WIN), 1) // CHUNK
    scale = HEAD_DIM ** -0.5
    chunks_per_blk = BAND_QBLK // CHUNK

    def masked_bias(hh, first_key_chunk_minus_query_chunk, tbl_off):
        rel = col_chunk + first_key_chunk_minus_query_chunk - row_chunk
        valid = (rel <= 0) & (rel >= -BAND_CHUNKS)
        return jnp.where(valid, tbl_ref[hh, :, tbl_off:tbl_off + BAND_WIN], NEG_BIG)

    def q_block(q0, k0, bias):
        q = q_ref[0, pl.ds(q0, BAND_QBLK), :] * scale
        zero = jnp.zeros_like(q)
        kb = k_ref[0, pl.ds(k0, BAND_WIN), :]
        vb = v_ref[0, pl.ds(k0, BAND_WIN), :]
        nh = HEADS_PER_LANE_TILE
        qh = jnp.concatenate([jnp.where(head_masks[hh], q, zero) for hh in range(nh)], axis=0)
        sc = lax.dot_general(qh, kb, (((1,), (1,)), ((), ())), preferred_element_type=F32)
        sc = sc + jnp.concatenate([bias(hh) for hh in range(nh)], axis=0)
        m = jnp.max(sc, axis=-1, keepdims=True)
        e = jnp.exp(sc - m)
        denom = jnp.sum(e, axis=-1, keepdims=True)
        oh = jnp.dot(e.astype(BF16), vb, preferred_element_type=F32) / denom
        out = oh[0:BAND_QBLK]
        for hh in range(1, nh):
            out = jnp.where(head_masks[hh], oh[hh * BAND_QBLK:(hh + 1) * BAND_QBLK], out)
        o_ref[0, pl.ds(q0, BAND_QBLK), :] = out.astype(BF16)

    n_head = lead // BAND_QBLK
    for i in range(n_head):
        q_block(i * BAND_QBLK, 0,
                lambda hh, i=i: masked_bias(hh, -i * chunks_per_blk, lead - i * BAND_QBLK))

    for hh in range(HEADS_PER_LANE_TILE):
        bias_ref[hh] = masked_bias(hh, -BAND_CHUNKS, 0)

    def body(t, _):
        for u in range(BAND_GROUP):
            q0 = pl.multiple_of((n_head + t * BAND_GROUP + u) * BAND_QBLK, BAND_QBLK)
            q_block(q0, pl.multiple_of(q0 - lead, BAND_QBLK), lambda hh: bias_ref[hh])
        return 0

    lax.fori_loop(0, (s // BAND_QBLK - n_head) // BAND_GROUP, body, 0)


def _band_bias_table(rel_bias):
    lead = BAND_CHUNKS * CHUNK
    span = BAND_TBL + BAND_QBLK
    n_tab = 2 * REL_CLIP + 1
    g = jnp.concatenate([jnp.repeat(rel_bias[:, -1:], lead, axis=1), rel_bias[:, ::-1],
                         jnp.repeat(rel_bias[:, :1], span - lead - n_tab, axis=1)], axis=1)
    period = jnp.pad(g, ((0, 0), (0, 1)))
    skew = jnp.tile(period, (1, BAND_QBLK))[:, :BAND_QBLK * span]
    return skew.reshape(-1, BAND_QBLK, span)[:, :, BAND_QBLK:].astype(F32)


def _band_attention(qkv, rel_bias):
    b, s, _ = qkv.shape
    np_ = N_HEAD_PAIRS
    tbl = _band_bias_table(rel_bias)
    return pl.pallas_call(
        _band_kernel,
        grid=(b, np_),
        in_specs=[
            pl.BlockSpec((1, s, LANES), lambda bi, p: (bi, 0, p)),
            pl.BlockSpec((1, s, LANES), lambda bi, p: (bi, 0, np_ + p)),
            pl.BlockSpec((1, s, LANES), lambda bi, p: (bi, 0, 2 * np_ + p)),
            pl.BlockSpec((HEADS_PER_LANE_TILE, BAND_QBLK, BAND_TBL), lambda bi, p: (p, 0, 0)),
        ],
        out_specs=pl.BlockSpec((1, s, LANES), lambda bi, p: (bi, 0, p)),
        out_shape=jax.ShapeDtypeStruct((b, s, D_MODEL), BF16),
        scratch_shapes=[pltpu.VMEM((HEADS_PER_LANE_TILE, BAND_QBLK, BAND_WIN), F32)],
        compiler_params=_cparams("parallel", "parallel"),
        name="band_attention",
    )(qkv, qkv, qkv, tbl)


def kernel(x, c, ada_w, ada_b, norm_g, ffn_w_in, ffn_w_out, sb_w_qkv, sb_w_o, sg_w_in, sg_ln_g, sg_w_s, sg_bias, sg_w_out, sc_w_in, sc_conv_w, sc_w_out, cb_w_qkv, cb_rel_bias, cb_w_o):
    assert x.shape[1] % ROW_TILE == 0 and x.shape[2] == D_MODEL
    mod_all = _ada_modulation(c, ada_w, ada_b)
    for i in range(DEPTH):
        m, r = i % N_MIXERS, i // N_MIXERS
        mod = mod_all[i]
        g_mix = norm_g[i, 0:2]
        if m == 0:
            qkv = _norm_mod_proj(x, mod, norm_g[i, 0], sb_w_qkv[r].astype(BF16), 0)
            o = _stick_breaking_attention(qkv)
            x = _proj_norm_residual(o, x, mod, norm_g[i, 1], sb_w_o[r].astype(BF16), 2)
        elif m == 1:
            x = _sgu_sublayer(x, mod, g_mix, sg_w_in[r], sg_ln_g[r], sg_w_s[r], sg_bias[r],
                              sg_w_out[r])
        elif m == 2:
            x = _conv_sublayer(x, mod, g_mix, sc_w_in[r], sc_conv_w[r], sc_w_out[r])
        else:
            qkv = _norm_mod_proj(x, mod, norm_g[i, 0], cb_w_qkv[r].astype(BF16), 0)
            o = _band_attention(qkv, cb_rel_bias[r])
            x = _proj_norm_residual(o, x, mod, norm_g[i, 1], cb_w_o[r].astype(BF16), 2)
        w_in_r, w_out_r = _ffn_weights(ffn_w_in[i], ffn_w_out[i])
        x = _ffn_sublayer(x, mod, norm_g[i, 2:4], w_in_r, w_out_r)
    return x
```

```python
import functools

import jax
import jax.numpy as jnp
from jax import lax
from jax.experimental import pallas as pl
from jax.experimental.pallas import tpu as pltpu

D_MODEL = 1024
DEPTH = 4
N_MIXERS = 4
N_HEADS = 16
HEAD_DIM = D_MODEL // N_HEADS
CHUNK = 64
GMLP_BLOCK = 128
GMLP_WIDTH = 2 * D_MODEL
GMLP_GROUPS = 8
GMLP_GROUP_CH = GMLP_WIDTH // GMLP_GROUPS
CONV_WIDTH = 3
BAND_CHUNKS = 8
REL_CLIP = 128
D_FF = 2816
EPS = 1e-6

LANES = 128
HEADS_PER_LANE_TILE = LANES // HEAD_DIM
N_HEAD_PAIRS = N_HEADS // HEADS_PER_LANE_TILE
VMEM_LIMIT = 56 * 1024 * 1024

ROW_TILE = 512
FFN_CHUNK = 512
SB_QBLK = 128
SB_KBLK = 256
SB_QSUP = 512
SB_CHAINS = SB_QSUP // SB_QBLK
SB_EXP_ZERO = 105.0
SB_BOUND_SLACK = 1.01
BAND_QBLK = 2 * CHUNK
BAND_GROUP = 4
BAND_WIN = (BAND_CHUNKS + 2) * CHUNK
BAND_TBL = BAND_WIN + BAND_CHUNKS * CHUNK
NEG_BIG = -1e30

BF16 = jnp.bfloat16
F32 = jnp.float32


def _cparams(*sem):
    return pltpu.CompilerParams(dimension_semantics=sem, vmem_limit_bytes=VMEM_LIMIT)


def _resident(shape):
    zeros = (0,) * len(shape)
    return pl.BlockSpec(shape, lambda *_: zeros, pipeline_mode=pl.Buffered(1))


def _rms_scale(v):
    return lax.rsqrt(jnp.mean(v * v, axis=-1, keepdims=True) + EPS)


def _prologue(x, g, shift, scale):
    y = x * _rms_scale(x) * g
    return (y * (1.0 + scale) + shift).astype(BF16)


def _epilogue(x, y, g, gate):
    return x + gate * (y * _rms_scale(y) * g)


def _ada_kernel(c_ref, w_ref, b_ref, o_ref):
    c = c_ref[...]
    a = (c * jax.nn.sigmoid(c)).astype(BF16)
    w = w_ref[0].astype(BF16)
    o_ref[0] = jnp.dot(a, w, preferred_element_type=F32) + b_ref[0]


def _ada_modulation(c, ada_w, ada_b):
    b = c.shape[0]
    rows = 8
    n = ada_w.shape[-1]
    tn = 1536
    c_pad = jnp.zeros((rows, D_MODEL), F32).at[:b].set(c)
    out = pl.pallas_call(
        _ada_kernel,
        grid=(DEPTH, n // tn),
        in_specs=[
            pl.BlockSpec((rows, D_MODEL), lambda l, j: (0, 0)),
            pl.BlockSpec((1, D_MODEL, tn), lambda l, j: (l, 0, j)),
            pl.BlockSpec((1, 1, tn), lambda l, j: (l, 0, j)),
        ],
        out_specs=pl.BlockSpec((1, rows, tn), lambda l, j: (l, 0, j)),
        out_shape=jax.ShapeDtypeStruct((DEPTH, rows, n), F32),
        compiler_params=_cparams("parallel", "parallel"),
        name="ada_modulation",
    )(c_pad, ada_w, ada_b.reshape(DEPTH, 1, n))
    return out[:, :b].reshape(DEPTH, b, 6, D_MODEL)


def _proj_kernel(x_ref, mod_ref, g_ref, w_ref, o_ref, *, mod_row, n_chunk):
    h = _prologue(x_ref[0], g_ref[...], mod_ref[0, mod_row:mod_row + 1, :],
                  mod_ref[0, mod_row + 1:mod_row + 2, :])
    n = w_ref.shape[1]
    for c in range(n // n_chunk):
        sl = slice(c * n_chunk, (c + 1) * n_chunk)
        o_ref[0, :, sl] = jnp.dot(h, w_ref[:, sl].astype(BF16),
                                  preferred_element_type=F32).astype(BF16)


def _norm_mod_proj(x, mod, g, w, mod_row):
    b, s, d = x.shape
    n = w.shape[1]
    return pl.pallas_call(
        functools.partial(_proj_kernel, mod_row=mod_row, n_chunk=512),
        grid=(b, s // ROW_TILE),
        in_specs=[
            pl.BlockSpec((1, ROW_TILE, d), lambda bi, i: (bi, i, 0)),
            pl.BlockSpec((1, 6, d), lambda bi, i: (bi, 0, 0)),
            _resident((1, d)),
            _resident((d, n)),
        ],
        out_specs=pl.BlockSpec((1, ROW_TILE, n), lambda bi, i: (bi, i, 0)),
        out_shape=jax.ShapeDtypeStruct((b, s, n), BF16),
        compiler_params=_cparams("parallel", "parallel"),
        name="norm_mod_proj",
    )(x, mod, g.reshape(1, d), w)


def _out_kernel(a_ref, x_ref, mod_ref, g_ref, w_ref, o_ref, *, gate_row):
    y = jnp.dot(a_ref[0], w_ref[...].astype(BF16), preferred_element_type=F32)
    o_ref[0] = _epilogue(x_ref[0], y, g_ref[...], mod_ref[0, gate_row:gate_row + 1, :])


def _proj_norm_residual(a, x, mod, g, w, gate_row):
    b, s, d = x.shape
    k = a.shape[-1]
    return pl.pallas_call(
        functools.partial(_out_kernel, gate_row=gate_row),
        grid=(b, s // ROW_TILE),
        in_specs=[
            pl.BlockSpec((1, ROW_TILE, k), lambda bi, i: (bi, i, 0)),
            pl.BlockSpec((1, ROW_TILE, d), lambda bi, i: (bi, i, 0)),
            pl.BlockSpec((1, 6, d), lambda bi, i: (bi, 0, 0)),
            _resident((1, d)),
            _resident((k, d)),
        ],
        out_specs=pl.BlockSpec((1, ROW_TILE, d), lambda bi, i: (bi, i, 0)),
        out_shape=jax.ShapeDtypeStruct((b, s, d), F32),
        compiler_params=_cparams("parallel", "parallel"),
        name="proj_norm_residual",
    )(a, x, mod, g.reshape(1, d), w)


def _ffn_kernel(x_ref, mod_ref, g_ref, win_ref, wout_ref, o_ref, acc_ref):
    x = x_ref[0]
    h = _prologue(x, g_ref[0:1, :], mod_ref[0, 3:4, :], mod_ref[0, 4:5, :])
    for c0 in range(0, D_FF, FFN_CHUNK):
        w = min(FFN_CHUNK, D_FF - c0)
        gate = jnp.dot(h, win_ref[:, c0:c0 + w].astype(BF16), preferred_element_type=F32)
        up = jnp.dot(h, win_ref[:, D_FF + c0:D_FF + c0 + w].astype(BF16),
                     preferred_element_type=F32)
        a = (gate * jax.nn.sigmoid(gate) * up).astype(BF16)
        part = jnp.dot(a, wout_ref[c0:c0 + w, :].astype(BF16), preferred_element_type=F32)
        if c0 == 0:
            acc_ref[...] = part
        else:
            acc_ref[...] += part
    o_ref[0] = _epilogue(x, acc_ref[...], g_ref[1:2, :], mod_ref[0, 5:6, :])


def _ffn_sublayer(x, mod, g2, w_in, w_out):
    b, s, d = x.shape
    return pl.pallas_call(
        _ffn_kernel,
        grid=(b, s // ROW_TILE),
        in_specs=[
            pl.BlockSpec((1, ROW_TILE, d), lambda bi, i: (bi, i, 0)),
            pl.BlockSpec((1, 6, d), lambda bi, i: (bi, 0, 0)),
            _resident((2, d)),
            _resident((d, 2 * D_FF)),
            _resident((D_FF, d)),
        ],
        out_specs=pl.BlockSpec((1, ROW_TILE, d), lambda bi, i: (bi, i, 0)),
        out_shape=jax.ShapeDtypeStruct((b, s, d), F32),
        scratch_shapes=[pltpu.VMEM((ROW_TILE, d), F32)],
        compiler_params=_cparams("parallel", "parallel"),
        name="ffn_sublayer",
    )(x, mod, g2, w_in, w_out)


def _gelu_tanh(x):
    c = 0.7978845608028654
    return 0.5 * x * (1.0 + jnp.tanh(c * (x + 0.044715 * (x * x * x))))


def _sgu_kernel(x_ref, mod_ref, g_ref, win_ref, lng_ref, ws_ref, sb_ref, wout_ref,
                o_ref, v_ref, y_ref):
    x = x_ref[0]
    h = _prologue(x, g_ref[0:1, :], mod_ref[0, 0:1, :], mod_ref[0, 1:2, :])
    n_chunk = 512
    half = GMLP_WIDTH // n_chunk
    for c in range(half):
        sl = slice(GMLP_WIDTH + c * n_chunk, GMLP_WIDTH + (c + 1) * n_chunk)
        v_ref[:, c * n_chunk:(c + 1) * n_chunk] = _gelu_tanh(
            jnp.dot(h, win_ref[:, sl].astype(BF16), preferred_element_type=F32))
    v = v_ref[...]
    mu = jnp.mean(v, axis=-1, keepdims=True)
    vc = v - mu
    var = jnp.mean(vc * vc, axis=-1, keepdims=True)
    vn = (vc * lax.rsqrt(var + EPS) * lng_ref[...]).astype(BF16)
    r = lax.broadcasted_iota(jnp.int32, (GMLP_BLOCK, GMLP_BLOCK), 0) // CHUNK
    cidx = lax.broadcasted_iota(jnp.int32, (GMLP_BLOCK, GMLP_BLOCK), 1) // CHUNK
    keep = r >= cidx
    for gi in range(GMLP_GROUPS):
        cs = slice(gi * GMLP_GROUP_CH, (gi + 1) * GMLP_GROUP_CH)
        u = _gelu_tanh(jnp.dot(h, win_ref[:, cs].astype(BF16), preferred_element_type=F32))
        ws = jnp.where(keep, ws_ref[gi], 0.0).astype(BF16)
        bias = sb_ref[:, gi:gi + 1]
        for blk in range(ROW_TILE // GMLP_BLOCK):
            rs = slice(blk * GMLP_BLOCK, (blk + 1) * GMLP_BLOCK)
            sv = jnp.dot(ws, vn[rs, cs], preferred_element_type=F32) + bias
            y_ref[rs, cs] = (u[rs] * sv).astype(BF16)
    y = jnp.dot(y_ref[...], wout_ref[...].astype(BF16), preferred_element_type=F32)
    o_ref[0] = _epilogue(x, y, g_ref[1:2, :], mod_ref[0, 2:3, :])


def _sgu_sublayer(x, mod, g2, w_in, ln_g, w_s, s_bias, w_out):
    b, s, d = x.shape
    return pl.pallas_call(
        _sgu_kernel,
        grid=(b, s // ROW_TILE),
        in_specs=[
            pl.BlockSpec((1, ROW_TILE, d), lambda bi, i: (bi, i, 0)),
            pl.BlockSpec((1, 6, d), lambda bi, i: (bi, 0, 0)),
            _resident((2, d)),
            _resident((d, 2 * GMLP_WIDTH)),
            _resident((1, GMLP_WIDTH)),
            _resident((GMLP_GROUPS, GMLP_BLOCK, GMLP_BLOCK)),
            _resident((GMLP_BLOCK, GMLP_GROUPS)),
            _resident((GMLP_WIDTH, d)),
        ],
        out_specs=pl.BlockSpec((1, ROW_TILE, d), lambda bi, i: (bi, i, 0)),
        out_shape=jax.ShapeDtypeStruct((b, s, d), F32),
        scratch_shapes=[pltpu.VMEM((ROW_TILE, GMLP_WIDTH), F32),
                        pltpu.VMEM((ROW_TILE, GMLP_WIDTH), BF16)],
        compiler_params=_cparams("parallel", "parallel"),
        name="sgu_sublayer",
    )(x, mod, g2, w_in, ln_g.reshape(1, GMLP_WIDTH), w_s, s_bias.T, w_out)


CONV_HALO = 8


def _conv_kernel(x_ref, mod_ref, g_ref, win_ref, cw_ref, wout_ref, o_ref, ext_ref, t_ref):
    i = pl.program_id(1)
    x = x_ref[0]
    h = _prologue(x, g_ref[0:1, :], mod_ref[0, 0:1, :], mod_ref[0, 1:2, :])

    @pl.when(i == 0)
    def _():
        ext_ref[0:CONV_HALO, :] = jnp.zeros((CONV_HALO, D_MODEL), F32)

    @pl.when(i != 0)
    def _():
        ext_ref[0:CONV_HALO, :] = ext_ref[ROW_TILE:ROW_TILE + CONV_HALO, :]

    n_chunk = 512
    for c in range(D_MODEL // n_chunk):
        cs = slice(c * n_chunk, (c + 1) * n_chunk)
        gate_c = jnp.dot(h, win_ref[:, D_MODEL + c * n_chunk:D_MODEL + (c + 1) * n_chunk]
                         .astype(BF16), preferred_element_type=F32)
        xt = jnp.dot(h, win_ref[:, 2 * D_MODEL + c * n_chunk:2 * D_MODEL + (c + 1) * n_chunk]
                     .astype(BF16), preferred_element_type=F32)
        ext_ref[CONV_HALO:, cs] = gate_c * xt
    for c in range(D_MODEL // n_chunk):
        cs = slice(c * n_chunk, (c + 1) * n_chunk)
        gate_b = jnp.dot(h, win_ref[:, cs].astype(BF16), preferred_element_type=F32)
        yc = (cw_ref[0:1, cs] * ext_ref[CONV_HALO - 2:CONV_HALO - 2 + ROW_TILE, cs]
              + cw_ref[1:2, cs] * ext_ref[CONV_HALO - 1:CONV_HALO - 1 + ROW_TILE, cs]
              + cw_ref[2:3, cs] * ext_ref[CONV_HALO:, cs])
        t_ref[:, cs] = (gate_b * yc).astype(BF16)
    y = jnp.dot(t_ref[...], wout_ref[...].astype(BF16), preferred_element_type=F32)
    o_ref[0] = _epilogue(x, y, g_ref[1:2, :], mod_ref[0, 2:3, :])


def _conv_sublayer(x, mod, g2, w_in, conv_w, w_out):
    b, s, d = x.shape
    return pl.pallas_call(
        _conv_kernel,
        grid=(b, s // ROW_TILE),
        in_specs=[
            pl.BlockSpec((1, ROW_TILE, d), lambda bi, i: (bi, i, 0)),
            pl.BlockSpec((1, 6, d), lambda bi, i: (bi, 0, 0)),
            _resident((2, d)),
            _resident((d, 3 * d)),
            _resident((CONV_WIDTH, d)),
            _resident((d, d)),
        ],
        out_specs=pl.BlockSpec((1, ROW_TILE, d), lambda bi, i: (bi, i, 0)),
        out_shape=jax.ShapeDtypeStruct((b, s, d), F32),
        scratch_shapes=[pltpu.VMEM((ROW_TILE + CONV_HALO, d), F32),
                        pltpu.VMEM((ROW_TILE, d), BF16)],
        compiler_params=_cparams("parallel", "arbitrary"),
        name="conv_sublayer",
    )(x, mod, g2, w_in, conv_w, w_out)


def _sb_kernel(q_ref, k_ref, v_ref, o_ref, qm_ref, acc_ref, run_ref, zmax_ref):
    s = q_ref.shape[1]
    nh = HEADS_PER_LANE_TILE
    kf = k_ref[0].astype(F32)
    k_norm_max = jnp.sqrt(jnp.max(jnp.sum(kf * kf, axis=-1, keepdims=True),
                                  axis=0, keepdims=True))
    lane = lax.broadcasted_iota(jnp.int32, (1, LANES), 1)
    head_masks = [(lane // HEAD_DIM == hh) for hh in range(nh)]
    tri = jnp.where(lax.broadcasted_iota(jnp.int32, (SB_KBLK, SB_KBLK), 0)
                    >= lax.broadcasted_iota(jnp.int32, (SB_KBLK, SB_KBLK), 1),
                    1.0, 0.0).astype(BF16)
    row = lax.broadcasted_iota(jnp.int32, (SB_QBLK, SB_KBLK), 0)
    col = lax.broadcasted_iota(jnp.int32, (SB_QBLK, SB_KBLK), 1)
    scale = HEAD_DIM ** -0.5

    rows_per_chunk = nh * SB_QBLK

    def block_step(k0, causal_of_chunk):
        c_lo = min(causal_of_chunk)
        r0 = c_lo * rows_per_chunk
        kb = k_ref[0, pl.ds(k0, SB_KBLK), :]
        vb = v_ref[0, pl.ds(k0, SB_KBLK), :]
        z = lax.dot_general(qm_ref[r0:, :], kb, (((1,), (1,)), ((), ())),
                             preferred_element_type=F32)
        chains = [(c, hh) for c in range(c_lo, SB_CHAINS) for hh in range(nh)]
        rows = [slice(n * SB_QBLK, (n + 1) * SB_QBLK) for n in range(len(chains))]
        sps = []
        for (c, _), rs in zip(chains, rows):
            zb = z[rs].astype(BF16)
            sp = jnp.maximum(zb, 0.0) + jnp.log(1.0 + jnp.exp(-jnp.abs(zb)))
            if causal_of_chunk[c] is not None:
                sp = jnp.where(causal_of_chunk[c], sp, jnp.zeros_like(sp))
            sps.append(sp)
        rest = jnp.dot(jnp.concatenate(sps, axis=0), tri, preferred_element_type=F32)
        run = run_ref[r0:, :]
        a_s = []
        for (c, _), rs in zip(chains, rows):
            run_c = jnp.concatenate([run[rs]] * (SB_KBLK // LANES), axis=1)
            a = jnp.exp(z[rs] - rest[rs] - run_c)
            if causal_of_chunk[c] is not None:
                a = jnp.where(causal_of_chunk[c], a, 0.0)
            a_s.append(a.astype(BF16))
        acc_ref[r0:, :] += jnp.dot(jnp.concatenate(a_s, axis=0), vb,
                                   preferred_element_type=F32)
        run_ref[r0:, :] = run + jnp.broadcast_to(rest[:, 0:1], run.shape)

    def q_super(g, _):
        q0 = pl.multiple_of(g * SB_QSUP, SB_QSUP)
        for c in range(SB_CHAINS):
            q = q_ref[0, pl.ds(q0 + c * SB_QBLK, SB_QBLK), :] * scale
            for hh in range(nh):
                r0 = c * rows_per_chunk + hh * SB_QBLK
                qm_ref[r0:r0 + SB_QBLK, :] = jnp.where(head_masks[hh], q, jnp.zeros_like(q))
        acc_ref[...] = jnp.zeros_like(acc_ref)
        run_ref[...] = jnp.zeros_like(run_ref)
        qf = qm_ref[...].astype(F32)
        q_norm = jnp.sqrt(jnp.sum(qf * qf, axis=-1, keepdims=True))
        zmax_ref[...] = jnp.broadcast_to(q_norm * k_norm_max * SB_BOUND_SLACK, zmax_ref.shape)

        def exit_margin():
            gap = run_ref[...] - zmax_ref[...]
            return jnp.min(gap, axis=0, keepdims=True)[0, 0]

        for jj in reversed(range(SB_QSUP // SB_KBLK)):
            first_key = jj * SB_KBLK
            causal_of_chunk = {}
            for c in range(SB_CHAINS):
                first_row = c * SB_QBLK
                if first_key >= first_row + SB_QBLK - 1:
                    continue
                full = first_key + SB_KBLK - 1 < first_row
                causal_of_chunk[c] = None if full else (col + first_key) < (row + first_row)
            assert sorted(causal_of_chunk) == list(range(min(causal_of_chunk), SB_CHAINS))
            block_step(q0 + first_key, causal_of_chunk)

        all_full = {c: None for c in range(SB_CHAINS)}

        n_far = g * (SB_QSUP // SB_KBLK)

        def more_keys_matter(carry):
            t, margin = carry
            return (t < n_far) & (margin < SB_EXP_ZERO)

        def key_block(carry):
            t, _ = carry
            block_step(pl.multiple_of(q0 - (t + 1) * SB_KBLK, SB_KBLK), all_full)
            return t + 1, exit_margin()

        lax.while_loop(more_keys_matter, key_block, (jnp.int32(0), exit_margin()))
        for c in range(SB_CHAINS):
            r0 = c * rows_per_chunk
            out = acc_ref[r0:r0 + SB_QBLK, :]
            for hh in range(1, nh):
                out = jnp.where(head_masks[hh],
                                acc_ref[r0 + hh * SB_QBLK:r0 + (hh + 1) * SB_QBLK, :], out)
            o_ref[0, pl.ds(q0 + c * SB_QBLK, SB_QBLK), :] = out.astype(BF16)
        return 0

    lax.fori_loop(0, s // SB_QSUP, q_super, 0)


def _stick_breaking_attention(qkv):
    b, s, _ = qkv.shape
    np_ = N_HEAD_PAIRS
    return pl.pallas_call(
        _sb_kernel,
        grid=(b, np_),
        in_specs=[
            pl.BlockSpec((1, s, LANES), lambda bi, p: (bi, 0, p)),
            pl.BlockSpec((1, s, LANES), lambda bi, p: (bi, 0, np_ + p)),
            pl.BlockSpec((1, s, LANES), lambda bi, p: (bi, 0, 2 * np_ + p)),
        ],
        out_specs=pl.BlockSpec((1, s, LANES), lambda bi, p: (bi, 0, p)),
        out_shape=jax.ShapeDtypeStruct((b, s, D_MODEL), BF16),
        scratch_shapes=[
            pltpu.VMEM((SB_QSUP * HEADS_PER_LANE_TILE, LANES), BF16),
            pltpu.VMEM((SB_QSUP * HEADS_PER_LANE_TILE, LANES), F32),
            pltpu.VMEM((SB_QSUP * HEADS_PER_LANE_TILE, LANES), F32),
            pltpu.VMEM((SB_QSUP * HEADS_PER_LANE_TILE, LANES), F32),
        ],
        compiler_params=_cparams("parallel", "parallel"),
        name="stick_breaking_attention",
    )(qkv, qkv, qkv)


def _band_kernel(q_ref, k_ref, v_ref, tbl_ref, o_ref, bias_ref):
    s = q_ref.shape[1]
    lead = BAND_CHUNKS * CHUNK
    lane = lax.broadcasted_iota(jnp.int32, (1, LANES), 1)
    head_masks = [(lane // HEAD_DIM == hh) for hh in range(HEADS_PER_LANE_TILE)]
    row_chunk = lax.broadcasted_iota(jnp.int32, (BAND_QBLK, BAND_WIN), 0) // CHUNK
    col_chunk = lax.broadcasted_iota(jnp.int32, (BAND_QBLK, BAND_WIN), 1) // CHUNK
    scale = HEAD_DIM ** -0.5
    chunks_per_blk = BAND_QBLK // CHUNK

    def masked_bias(hh, first_key_chunk_minus_query_chunk, tbl_off):
        rel = col_chunk + first_key_chunk_minus_query_chunk - row_chunk
        valid = (rel <= 0) & (rel >= -BAND_CHUNKS)
        return jnp.where(valid, tbl_ref[hh, :, tbl_off:tbl_off + BAND_WIN], NEG_BIG)

    def q_block(q0, k0, bias):
        q = q_ref[0, pl.ds(q0, BAND_QBLK), :] * scale
        zero = jnp.zeros_like(q)
        kb = k_ref[0, pl.ds(k0, BAND_WIN), :]
        vb = v_ref[0, pl.ds(k0, BAND_WIN), :]
        nh = HEADS_PER_LANE_TILE
        qh = jnp.concatenate([jnp.where(head_masks[hh], q, zero) for hh in range(nh)], axis=0)
        sc = lax.dot_general(qh, kb, (((1,), (1,)), ((), ())), preferred_element_type=F32)
        sc = sc + jnp.concatenate([bias(hh) for hh in range(nh)], axis=0)
        m = jnp.max(sc, axis=-1, keepdims=True)
        e = jnp.exp(sc - m)
        denom = jnp.sum(e, axis=-1, keepdims=True)
        oh = jnp.dot(e.astype(BF16), vb, preferred_element_type=F32) / denom
        out = oh[0:BAND_QBLK]
        for hh in range(1, nh):
            out = jnp.where(head_masks[hh], oh[hh * BAND_QBLK:(hh + 1) * BAND_QBLK], out)
        o_ref[0, pl.ds(q0, BAND_QBLK), :] = out.astype(BF16)

    n_head = lead // BAND_QBLK
    for i in range(n_head):
        q_block(i * BAND_QBLK, 0,
                lambda hh, i=i: masked_bias(hh, -i * chunks_per_blk, lead - i * BAND_QBLK))

    for hh in range(HEADS_PER_LANE_TILE):
        bias_ref[hh] = masked_bias(hh, -BAND_CHUNKS, 0)

    def body(t, _):
        for u in range(BAND_GROUP):
            q0 = pl.multiple_of((n_head + t * BAND_GROUP + u) * BAND_QBLK, BAND_QBLK)
            q_block(q0, pl.multiple_of(q0 - lead, BAND_QBLK), lambda hh: bias_ref[hh])
        return 0

    lax.fori_loop(0, (s // BAND_QBLK - n_head) // BAND_GROUP, body, 0)


def _band_bias_table(rel_bias):
    lead = BAND_CHUNKS * CHUNK
    span = BAND_TBL + BAND_QBLK
    n_tab = 2 * REL_CLIP + 1
    g = jnp.concatenate([jnp.repeat(rel_bias[:, -1:], lead, axis=1), rel_bias[:, ::-1],
                         jnp.repeat(rel_bias[:, :1], span - lead - n_tab, axis=1)], axis=1)
    period = jnp.pad(g, ((0, 0), (0, 1)))
    skew = jnp.tile(period, (1, BAND_QBLK))[:, :BAND_QBLK * span]
    return skew.reshape(-1, BAND_QBLK, span)[:, :, BAND_QBLK:].astype(F32)


def _band_attention(qkv, rel_bias):
    b, s, _ = qkv.shape
    np_ = N_HEAD_PAIRS
    tbl = _band_bias_table(rel_bias)
    return pl.pallas_call(
        _band_kernel,
        grid=(b, np_),
        in_specs=[
            pl.BlockSpec((1, s, LANES), lambda bi, p: (bi, 0, p)),
            pl.BlockSpec((1, s, LANES), lambda bi, p: (bi, 0, np_ + p)),
            pl.BlockSpec((1, s, LANES), lambda bi, p: (bi, 0, 2 * np_ + p)),
            pl.BlockSpec((HEADS_PER_LANE_TILE, BAND_QBLK, BAND_TBL), lambda bi, p: (p, 0, 0)),
        ],
        out_specs=pl.BlockSpec((1, s, LANES), lambda bi, p: (bi, 0, p)),
        out_shape=jax.ShapeDtypeStruct((b, s, D_MODEL), BF16),
        scratch_shapes=[pltpu.VMEM((HEADS_PER_LANE_TILE, BAND_QBLK, BAND_WIN), F32)],
        compiler_params=_cparams("parallel", "parallel"),
        name="band_attention",
    )(qkv, qkv, qkv, tbl)


def kernel(x, c, ada_w, ada_b, norm_g, ffn_w_in, ffn_w_out, sb_w_qkv, sb_w_o, sg_w_in, sg_ln_g, sg_w_s, sg_bias, sg_w_out, sc_w_in, sc_conv_w, sc_w_out, cb_w_qkv, cb_rel_bias, cb_w_o):
    assert x.shape[1] % ROW_TILE == 0 and x.shape[2] == D_MODEL
    mod_all = _ada_modulation(c, ada_w, ada_b)
    for i in range(DEPTH):
        m, r = i % N_MIXERS, i // N_MIXERS
        mod = mod_all[i]
        g_mix = norm_g[i, 0:2]
        if m == 0:
            qkv = _norm_mod_proj(x, mod, norm_g[i, 0], sb_w_qkv[r], 0)
            o = _stick_breaking_attention(qkv)
            x = _proj_norm_residual(o, x, mod, norm_g[i, 1], sb_w_o[r], 2)
        elif m == 1:
            x = _sgu_sublayer(x, mod, g_mix, sg_w_in[r], sg_ln_g[r], sg_w_s[r], sg_bias[r],
                              sg_w_out[r])
        elif m == 2:
            x = _conv_sublayer(x, mod, g_mix, sc_w_in[r], sc_conv_w[r], sc_w_out[r])
        else:
            qkv = _norm_mod_proj(x, mod, norm_g[i, 0], cb_w_qkv[r], 0)
            o = _band_attention(qkv, cb_rel_bias[r])
            x = _proj_norm_residual(o, x, mod, norm_g[i, 1], cb_w_o[r], 2)
        x = _ffn_sublayer(x, mod, norm_g[i, 2:4], ffn_w_in[i], ffn_w_out[i])
    return x
```

```python
import functools
import math

import jax
import jax.numpy as jnp
from jax import lax
from jax.experimental import pallas as pl
from jax.experimental.pallas import tpu as pltpu

D_MODEL = 1024
DEPTH = 4
N_MIXERS = 4
N_HEADS = 16
HEAD_DIM = D_MODEL // N_HEADS
CHUNK = 64
GMLP_BLOCK = 128
GMLP_WIDTH = 2 * D_MODEL
GMLP_GROUPS = 8
GMLP_GROUP_CH = GMLP_WIDTH // GMLP_GROUPS
CONV_WIDTH = 3
BAND_CHUNKS = 8
REL_CLIP = 128
D_FF = 2816
EPS = 1e-6

LANES = 128
HEADS_PER_LANE_TILE = LANES // HEAD_DIM
N_HEAD_PAIRS = N_HEADS // HEADS_PER_LANE_TILE
VMEM_LIMIT = 56 * 1024 * 1024

ROW_TILE = 512
FFN_CHUNK = 512
SB_QBLK = 128
SB_KBLK = 256
SB_QSUP = 512
SB_CHAINS = SB_QSUP // SB_QBLK
SB_EXP_ZERO = 105.0
SB_BOUND_SLACK = 1.01
BAND_QBLK = 4 * CHUNK
BAND_GROUP = 14
BAND_WIN = BAND_CHUNKS * CHUNK + BAND_QBLK
BAND_TBL = BAND_WIN + BAND_CHUNKS * CHUNK
NEG_BIG = -1e30

BF16 = jnp.bfloat16
F32 = jnp.float32


def _cparams(*sem):
    return pltpu.CompilerParams(dimension_semantics=sem, vmem_limit_bytes=VMEM_LIMIT)


def _resident(shape):
    zeros = (0,) * len(shape)
    return pl.BlockSpec(shape, lambda *_: zeros, pipeline_mode=pl.Buffered(1))


def _resident_layer(shape, layer):
    index = (layer,) + (0,) * len(shape)
    return pl.BlockSpec((pl.Squeezed(),) + tuple(shape), lambda *_: index,
                        pipeline_mode=pl.Buffered(1))


def _rms_scale(v):
    return lax.rsqrt(jnp.mean(v * v, axis=-1, keepdims=True) + EPS)


def _prologue(x, g, shift, scale):
    y = x * _rms_scale(x) * g
    return (y * (1.0 + scale) + shift).astype(BF16)


def _epilogue(x, y, g, gate):
    return x + gate * (y * _rms_scale(y) * g)


def _ada_kernel(c_ref, w_ref, b_ref, o_ref):
    c = c_ref[...]
    a = (c * jax.nn.sigmoid(c)).astype(BF16)
    w = w_ref[0].astype(BF16)
    o_ref[0] = jnp.dot(a, w, preferred_element_type=F32) + b_ref[0]


def _ada_modulation(c, ada_w, ada_b):
    b = c.shape[0]
    rows = 8
    n = ada_w.shape[-1]
    tn = 1536
    c_pad = jnp.zeros((rows, D_MODEL), F32).at[:b].set(c)
    out = pl.pallas_call(
        _ada_kernel,
        grid=(DEPTH, n // tn),
        in_specs=[
            pl.BlockSpec((rows, D_MODEL), lambda l, j: (0, 0)),
            pl.BlockSpec((1, D_MODEL, tn), lambda l, j: (l, 0, j)),
            pl.BlockSpec((1, 1, tn), lambda l, j: (l, 0, j)),
        ],
        out_specs=pl.BlockSpec((1, rows, tn), lambda l, j: (l, 0, j)),
        out_shape=jax.ShapeDtypeStruct((DEPTH, rows, n), F32),
        compiler_params=_cparams("parallel", "parallel"),
        name="ada_modulation",
    )(c_pad, ada_w, ada_b.reshape(DEPTH, 1, n))
    return out[:, :b].reshape(DEPTH, b, 6, D_MODEL)


def _proj_kernel(x_ref, mod_ref, g_ref, w_ref, o_ref, *, mod_row, n_chunk):
    h = _prologue(x_ref[0], g_ref[...], mod_ref[0, mod_row:mod_row + 1, :],
                  mod_ref[0, mod_row + 1:mod_row + 2, :])
    n = w_ref.shape[1]
    for c in range(n // n_chunk):
        sl = slice(c * n_chunk, (c + 1) * n_chunk)
        o_ref[0, :, sl] = jnp.dot(h, w_ref[:, sl].astype(BF16),
                                  preferred_element_type=F32).astype(BF16)


def _norm_mod_proj(x, mod, g, w, mod_row):
    b, s, d = x.shape
    n = w.shape[1]
    return pl.pallas_call(
        functools.partial(_proj_kernel, mod_row=mod_row, n_chunk=512),
        grid=(b, s // ROW_TILE),
        in_specs=[
            pl.BlockSpec((1, ROW_TILE, d), lambda bi, i: (bi, i, 0)),
            pl.BlockSpec((1, 6, d), lambda bi, i: (bi, 0, 0)),
            _resident((1, d)),
            _resident((d, n)),
        ],
        out_specs=pl.BlockSpec((1, ROW_TILE, n), lambda bi, i: (bi, i, 0)),
        out_shape=jax.ShapeDtypeStruct((b, s, n), BF16),
        compiler_params=_cparams("parallel", "parallel"),
        name="norm_mod_proj",
    )(x, mod, g.reshape(1, d), w)


def _out_kernel(a_ref, x_ref, mod_ref, g_ref, w_ref, o_ref, *, gate_row):
    y = jnp.dot(a_ref[0], w_ref[...].astype(BF16), preferred_element_type=F32)
    o_ref[0] = _epilogue(x_ref[0], y, g_ref[...], mod_ref[0, gate_row:gate_row + 1, :])


def _proj_norm_residual(a, x, mod, g, w, gate_row):
    b, s, d = x.shape
    k = a.shape[-1]
    return pl.pallas_call(
        functools.partial(_out_kernel, gate_row=gate_row),
        grid=(b, s // ROW_TILE),
        in_specs=[
            pl.BlockSpec((1, ROW_TILE, k), lambda bi, i: (bi, i, 0)),
            pl.BlockSpec((1, ROW_TILE, d), lambda bi, i: (bi, i, 0)),
            pl.BlockSpec((1, 6, d), lambda bi, i: (bi, 0, 0)),
            _resident((1, d)),
            _resident((k, d)),
        ],
        out_specs=pl.BlockSpec((1, ROW_TILE, d), lambda bi, i: (bi, i, 0)),
        out_shape=jax.ShapeDtypeStruct((b, s, d), F32),
        compiler_params=_cparams("parallel", "parallel"),
        name="proj_norm_residual",
    )(a, x, mod, g.reshape(1, d), w)


def _ffn_kernel(x_ref, mod_ref, g_ref, win_ref, wout_ref, o_ref, acc_ref):
    x = x_ref[0]
    h = _prologue(x, g_ref[0:1, :], mod_ref[0, 3:4, :], mod_ref[0, 4:5, :])
    for c0 in range(0, D_FF, FFN_CHUNK):
        w = min(FFN_CHUNK, D_FF - c0)
        gate = jnp.dot(h, win_ref[:, c0:c0 + w].astype(BF16), preferred_element_type=F32)
        up = jnp.dot(h, win_ref[:, D_FF + c0:D_FF + c0 + w].astype(BF16),
                     preferred_element_type=F32)
        a = (gate * jax.nn.sigmoid(gate) * up).astype(BF16)
        part = jnp.dot(a, wout_ref[c0:c0 + w, :].astype(BF16), preferred_element_type=F32)
        if c0 == 0:
            acc_ref[...] = part
        else:
            acc_ref[...] += part
    o_ref[0] = _epilogue(x, acc_ref[...], g_ref[1:2, :], mod_ref[0, 5:6, :])


def _ffn_sublayer(x, mod, g2, w_in, w_out, layer):
    b, s, d = x.shape
    return pl.pallas_call(
        _ffn_kernel,
        grid=(b, s // ROW_TILE),
        in_specs=[
            pl.BlockSpec((1, ROW_TILE, d), lambda bi, i: (bi, i, 0)),
            pl.BlockSpec((1, 6, d), lambda bi, i: (bi, 0, 0)),
            _resident((2, d)),
            _resident_layer((d, 2 * D_FF), layer),
            _resident_layer((D_FF, d), layer),
        ],
        out_specs=pl.BlockSpec((1, ROW_TILE, d), lambda bi, i: (bi, i, 0)),
        out_shape=jax.ShapeDtypeStruct((b, s, d), F32),
        scratch_shapes=[pltpu.VMEM((ROW_TILE, d), F32)],
        compiler_params=_cparams("parallel", "parallel"),
        name="ffn_sublayer",
    )(x, mod, g2, w_in, w_out)


def _gelu_tanh(x):
    c = 0.7978845608028654
    return 0.5 * x * (1.0 + jnp.tanh(c * (x + 0.044715 * (x * x * x))))


def _sgu_kernel(x_ref, mod_ref, g_ref, win_ref, lng_ref, ws_ref, sb_ref, wout_ref,
                o_ref, v_ref, y_ref):
    x = x_ref[0]
    h = _prologue(x, g_ref[0:1, :], mod_ref[0, 0:1, :], mod_ref[0, 1:2, :])
    n_chunk = 512
    half = GMLP_WIDTH // n_chunk
    for c in range(half):
        sl = slice(GMLP_WIDTH + c * n_chunk, GMLP_WIDTH + (c + 1) * n_chunk)
        v_ref[:, c * n_chunk:(c + 1) * n_chunk] = _gelu_tanh(
            jnp.dot(h, win_ref[:, sl].astype(BF16), preferred_element_type=F32))
    v = v_ref[...]
    mu = jnp.mean(v, axis=-1, keepdims=True)
    vc = v - mu
    var = jnp.mean(vc * vc, axis=-1, keepdims=True)
    vn = (vc * lax.rsqrt(var + EPS) * lng_ref[...]).astype(BF16)
    r = lax.broadcasted_iota(jnp.int32, (GMLP_BLOCK, GMLP_BLOCK), 0) // CHUNK
    cidx = lax.broadcasted_iota(jnp.int32, (GMLP_BLOCK, GMLP_BLOCK), 1) // CHUNK
    keep = r >= cidx
    for gi in range(GMLP_GROUPS):
        cs = slice(gi * GMLP_GROUP_CH, (gi + 1) * GMLP_GROUP_CH)
        u = _gelu_tanh(jnp.dot(h, win_ref[:, cs].astype(BF16), preferred_element_type=F32))
        ws = jnp.where(keep, ws_ref[gi], 0.0).astype(BF16)
        bias = sb_ref[:, gi:gi + 1]
        for blk in range(ROW_TILE // GMLP_BLOCK):
            rs = slice(blk * GMLP_BLOCK, (blk + 1) * GMLP_BLOCK)
            sv = jnp.dot(ws, vn[rs, cs], preferred_element_type=F32) + bias
            y_ref[rs, cs] = (u[rs] * sv).astype(BF16)
    y = jnp.dot(y_ref[...], wout_ref[...].astype(BF16), preferred_element_type=F32)
    o_ref[0] = _epilogue(x, y, g_ref[1:2, :], mod_ref[0, 2:3, :])


def _sgu_sublayer(x, mod, g2, w_in, ln_g, w_s, s_bias, w_out):
    b, s, d = x.shape
    return pl.pallas_call(
        _sgu_kernel,
        grid=(b, s // ROW_TILE),
        in_specs=[
            pl.BlockSpec((1, ROW_TILE, d), lambda bi, i: (bi, i, 0)),
            pl.BlockSpec((1, 6, d), lambda bi, i: (bi, 0, 0)),
            _resident((2, d)),
            _resident((d, 2 * GMLP_WIDTH)),
            _resident((1, GMLP_WIDTH)),
            _resident((GMLP_GROUPS, GMLP_BLOCK, GMLP_BLOCK)),
            _resident((GMLP_BLOCK, GMLP_GROUPS)),
            _resident((GMLP_WIDTH, d)),
        ],
        out_specs=pl.BlockSpec((1, ROW_TILE, d), lambda bi, i: (bi, i, 0)),
        out_shape=jax.ShapeDtypeStruct((b, s, d), F32),
        scratch_shapes=[pltpu.VMEM((ROW_TILE, GMLP_WIDTH), F32),
                        pltpu.VMEM((ROW_TILE, GMLP_WIDTH), BF16)],
        compiler_params=_cparams("parallel", "parallel"),
        name="sgu_sublayer",
    )(x, mod, g2, w_in, ln_g.reshape(1, GMLP_WIDTH), w_s, s_bias.T, w_out)


CONV_HALO = 8


def _conv_kernel(x_ref, mod_ref, g_ref, win_ref, cw_ref, wout_ref, o_ref, ext_ref, t_ref):
    i = pl.program_id(1)
    x = x_ref[0]
    h = _prologue(x, g_ref[0:1, :], mod_ref[0, 0:1, :], mod_ref[0, 1:2, :])

    @pl.when(i == 0)
    def _():
        ext_ref[0:CONV_HALO, :] = jnp.zeros((CONV_HALO, D_MODEL), F32)

    @pl.when(i != 0)
    def _():
        ext_ref[0:CONV_HALO, :] = ext_ref[ROW_TILE:ROW_TILE + CONV_HALO, :]

    n_chunk = 512
    for c in range(D_MODEL // n_chunk):
        cs = slice(c * n_chunk, (c + 1) * n_chunk)
        gate_c = jnp.dot(h, win_ref[:, D_MODEL + c * n_chunk:D_MODEL + (c + 1) * n_chunk]
                         .astype(BF16), preferred_element_type=F32)
        xt = jnp.dot(h, win_ref[:, 2 * D_MODEL + c * n_chunk:2 * D_MODEL + (c + 1) * n_chunk]
                     .astype(BF16), preferred_element_type=F32)
        ext_ref[CONV_HALO:, cs] = gate_c * xt
    for c in range(D_MODEL // n_chunk):
        cs = slice(c * n_chunk, (c + 1) * n_chunk)
        gate_b = jnp.dot(h, win_ref[:, cs].astype(BF16), preferred_element_type=F32)
        yc = (cw_ref[0:1, cs] * ext_ref[CONV_HALO - 2:CONV_HALO - 2 + ROW_TILE, cs]
              + cw_ref[1:2, cs] * ext_ref[CONV_HALO - 1:CONV_HALO - 1 + ROW_TILE, cs]
              + cw_ref[2:3, cs] * ext_ref[CONV_HALO:, cs])
        t_ref[:, cs] = (gate_b * yc).astype(BF16)
    y = jnp.dot(t_ref[...], wout_ref[...].astype(BF16), preferred_element_type=F32)
    o_ref[0] = _epilogue(x, y, g_ref[1:2, :], mod_ref[0, 2:3, :])


def _conv_sublayer(x, mod, g2, w_in, conv_w, w_out):
    b, s, d = x.shape
    return pl.pallas_call(
        _conv_kernel,
        grid=(b, s // ROW_TILE),
        in_specs=[
            pl.BlockSpec((1, ROW_TILE, d), lambda bi, i: (bi, i, 0)),
            pl.BlockSpec((1, 6, d), lambda bi, i: (bi, 0, 0)),
            _resident((2, d)),
            _resident((d, 3 * d)),
            _resident((CONV_WIDTH, d)),
            _resident((d, d)),
        ],
        out_specs=pl.BlockSpec((1, ROW_TILE, d), lambda bi, i: (bi, i, 0)),
        out_shape=jax.ShapeDtypeStruct((b, s, d), F32),
        scratch_shapes=[pltpu.VMEM((ROW_TILE + CONV_HALO, d), F32),
                        pltpu.VMEM((ROW_TILE, d), BF16)],
        compiler_params=_cparams("parallel", "arbitrary"),
        name="conv_sublayer",
    )(x, mod, g2, w_in, conv_w, w_out)


def _sb_kernel(q_ref, k_ref, v_ref, o_ref, qm_ref, acc_ref, run_ref, zmax_ref):
    s = q_ref.shape[1]
    nh = HEADS_PER_LANE_TILE
    kf = k_ref[0].astype(F32)
    k_norm_max = jnp.sqrt(jnp.max(jnp.sum(kf * kf, axis=-1, keepdims=True),
                                  axis=0, keepdims=True))
    lane = lax.broadcasted_iota(jnp.int32, (1, LANES), 1)
    head_masks = [(lane // HEAD_DIM == hh) for hh in range(nh)]
    tri = jnp.where(lax.broadcasted_iota(jnp.int32, (SB_KBLK, SB_KBLK), 0)
                    >= lax.broadcasted_iota(jnp.int32, (SB_KBLK, SB_KBLK), 1),
                    1.0, 0.0).astype(BF16)
    row = lax.broadcasted_iota(jnp.int32, (SB_QBLK, SB_KBLK), 0)
    col = lax.broadcasted_iota(jnp.int32, (SB_QBLK, SB_KBLK), 1)
    scale = HEAD_DIM ** -0.5

    rows_per_chunk = nh * SB_QBLK

    def block_step(k0, causal_of_chunk):
        c_lo = min(causal_of_chunk)
        r0 = c_lo * rows_per_chunk
        kb = k_ref[0, pl.ds(k0, SB_KBLK), :]
        vb = v_ref[0, pl.ds(k0, SB_KBLK), :]
        z = lax.dot_general(qm_ref[r0:, :], kb, (((1,), (1,)), ((), ())),
                             preferred_element_type=F32)
        chains = [(c, hh) for c in range(c_lo, SB_CHAINS) for hh in range(nh)]
        rows = [slice(n * SB_QBLK, (n + 1) * SB_QBLK) for n in range(len(chains))]
        sps = []
        for (c, _), rs in zip(chains, rows):
            zb = z[rs].astype(BF16)
            sp = jnp.maximum(zb, 0.0) + jnp.log(1.0 + jnp.exp(-jnp.abs(zb)))
            if causal_of_chunk[c] is not None:
                sp = jnp.where(causal_of_chunk[c], sp, jnp.zeros_like(sp))
            sps.append(sp)
        rest = jnp.dot(jnp.concatenate(sps, axis=0), tri, preferred_element_type=F32)
        run = run_ref[r0:, :]
        a_s = []
        for (c, _), rs in zip(chains, rows):
            run_c = jnp.concatenate([run[rs]] * (SB_KBLK // LANES), axis=1)
            a = jnp.exp(z[rs] - rest[rs] - run_c)
            if causal_of_chunk[c] is not None:
                a = jnp.where(causal_of_chunk[c], a, 0.0)
            a_s.append(a.astype(BF16))
        acc_ref[r0:, :] += jnp.dot(jnp.concatenate(a_s, axis=0), vb,
                                   preferred_element_type=F32)
        run_ref[r0:, :] = run + jnp.broadcast_to(rest[:, 0:1], run.shape)

    def q_super(g, _):
        q0 = pl.multiple_of(g * SB_QSUP, SB_QSUP)
        for c in range(SB_CHAINS):
            q = q_ref[0, pl.ds(q0 + c * SB_QBLK, SB_QBLK), :] * scale
            for hh in range(nh):
                r0 = c * rows_per_chunk + hh * SB_QBLK
                qm_ref[r0:r0 + SB_QBLK, :] = jnp.where(head_masks[hh], q, jnp.zeros_like(q))
        acc_ref[...] = jnp.zeros_like(acc_ref)
        run_ref[...] = jnp.zeros_like(run_ref)
        qf = qm_ref[...].astype(F32)
        q_norm = jnp.sqrt(jnp.sum(qf * qf, axis=-1, keepdims=True))
        zmax_ref[...] = jnp.broadcast_to(q_norm * k_norm_max * SB_BOUND_SLACK, zmax_ref.shape)

        def exit_margin():
            gap = run_ref[...] - zmax_ref[...]
            return jnp.min(gap, axis=0, keepdims=True)[0, 0]

        for jj in reversed(range(SB_QSUP // SB_KBLK)):
            first_key = jj * SB_KBLK
            causal_of_chunk = {}
            for c in range(SB_CHAINS):
                first_row = c * SB_QBLK
                if first_key >= first_row + SB_QBLK - 1:
                    continue
                full = first_key + SB_KBLK - 1 < first_row
                causal_of_chunk[c] = None if full else (col + first_key) < (row + first_row)
            assert sorted(causal_of_chunk) == list(range(min(causal_of_chunk), SB_CHAINS))
            block_step(q0 + first_key, causal_of_chunk)

        all_full = {c: None for c in range(SB_CHAINS)}

        n_far = g * (SB_QSUP // SB_KBLK)

        def more_keys_matter(carry):
            t, margin = carry
            return (t < n_far) & (margin < SB_EXP_ZERO)

        def key_block(carry):
            t, _ = carry
            block_step(pl.multiple_of(q0 - (t + 1) * SB_KBLK, SB_KBLK), all_full)
            return t + 1, exit_margin()

        lax.while_loop(more_keys_matter, key_block, (jnp.int32(0), exit_margin()))
        for c in range(SB_CHAINS):
            r0 = c * rows_per_chunk
            out = acc_ref[r0:r0 + SB_QBLK, :]
            for hh in range(1, nh):
                out = jnp.where(head_masks[hh],
                                acc_ref[r0 + hh * SB_QBLK:r0 + (hh + 1) * SB_QBLK, :], out)
            o_ref[0, pl.ds(q0 + c * SB_QBLK, SB_QBLK), :] = out.astype(BF16)
        return 0

    lax.fori_loop(0, s // SB_QSUP, q_super, 0)


def _stick_breaking_attention(qkv):
    b, s, _ = qkv.shape
    np_ = N_HEAD_PAIRS
    return pl.pallas_call(
        _sb_kernel,
        grid=(b, np_),
        in_specs=[
            pl.BlockSpec((1, s, LANES), lambda bi, p: (bi, 0, p)),
            pl.BlockSpec((1, s, LANES), lambda bi, p: (bi, 0, np_ + p)),
            pl.BlockSpec((1, s, LANES), lambda bi, p: (bi, 0, 2 * np_ + p)),
        ],
        out_specs=pl.BlockSpec((1, s, LANES), lambda bi, p: (bi, 0, p)),
        out_shape=jax.ShapeDtypeStruct((b, s, D_MODEL), BF16),
        scratch_shapes=[
            pltpu.VMEM((SB_QSUP * HEADS_PER_LANE_TILE, LANES), BF16),
            pltpu.VMEM((SB_QSUP * HEADS_PER_LANE_TILE, LANES), F32),
            pltpu.VMEM((SB_QSUP * HEADS_PER_LANE_TILE, LANES), F32),
            pltpu.VMEM((SB_QSUP * HEADS_PER_LANE_TILE, LANES), F32),
        ],
        compiler_params=_cparams("parallel", "parallel"),
        name="stick_breaking_attention",
    )(qkv, qkv, qkv)


def _band_kernel(q_ref, k_ref, v_ref, g_ref, o_ref, tbl_ref, bias_ref):
    s = q_ref.shape[1]
    lead = BAND_CHUNKS * CHUNK
    nh = HEADS_PER_LANE_TILE
    n_head = lead // BAND_QBLK
    chunks_per_blk = BAND_QBLK // CHUNK
    lane = lax.broadcasted_iota(jnp.int32, (1, LANES), 1)
    head_masks = [(lane // HEAD_DIM == hh) for hh in range(nh)]
    scale = HEAD_DIM ** -0.5

    @pl.when(pl.program_id(1) == 0)
    def _():
        for hh in range(nh):
            g_rows = jnp.broadcast_to(g_ref[hh], (BAND_QBLK, BAND_TBL + BAND_QBLK))
            tbl_ref[hh] = pltpu.roll(g_rows, 0, 1, stride=1, stride_axis=0)[:, BAND_QBLK:]
        row_chunk = lax.broadcasted_iota(jnp.int32, (BAND_QBLK, BAND_WIN), 0) // CHUNK
        col_chunk = lax.broadcasted_iota(jnp.int32, (BAND_QBLK, BAND_WIN), 1) // CHUNK
        for slot in range(n_head + 1):
            first_key_chunk_minus_query_chunk = -min(slot * chunks_per_blk, BAND_CHUNKS)
            tbl_off = lead + first_key_chunk_minus_query_chunk * CHUNK
            rel = col_chunk + first_key_chunk_minus_query_chunk - row_chunk
            valid = (rel <= 0) & (rel >= -BAND_CHUNKS)
            for hh in range(nh):
                bias_ref[slot * nh + hh] = jnp.where(
                    valid, tbl_ref[hh, :, tbl_off:tbl_off + BAND_WIN], NEG_BIG)

    def q_block(q0, k0, slot):
        q = q_ref[0, pl.ds(q0, BAND_QBLK), :] * scale
        zero = jnp.zeros_like(q)
        kb = k_ref[0, pl.ds(k0, BAND_WIN), :]
        vb = v_ref[0, pl.ds(k0, BAND_WIN), :]
        qh = jnp.concatenate([jnp.where(head_masks[hh], q, zero) for hh in range(nh)], axis=0)
        sc = lax.dot_general(qh, kb, (((1,), (1,)), ((), ())), preferred_element_type=F32)
        sc = sc + jnp.concatenate([bias_ref[slot * nh + hh] for hh in range(nh)], axis=0)
        m = jnp.max(sc, axis=-1, keepdims=True)
        e = jnp.exp(sc - m)
        denom = jnp.sum(e, axis=-1, keepdims=True)
        oh = jnp.dot(e.astype(BF16), vb, preferred_element_type=F32) / denom
        out = oh[0:BAND_QBLK]
        for hh in range(1, nh):
            out = jnp.where(head_masks[hh], oh[hh * BAND_QBLK:(hh + 1) * BAND_QBLK], out)
        o_ref[0, pl.ds(q0, BAND_QBLK), :] = out.astype(BF16)

    for i in range(n_head):
        q_block(i * BAND_QBLK, 0, i)

    n_general = s // BAND_QBLK - n_head
    group = math.gcd(n_general, BAND_GROUP)

    def body(t, _):
        for u in range(group):
            q0 = pl.multiple_of((n_head + t * group + u) * BAND_QBLK, BAND_QBLK)
            q_block(q0, pl.multiple_of(q0 - lead, BAND_QBLK), n_head)
        return 0

    lax.fori_loop(0, n_general // group, body, 0)


def _band_bias_vector(rel_bias):
    lead = BAND_CHUNKS * CHUNK
    span = BAND_TBL + BAND_QBLK
    n_tab = 2 * REL_CLIP + 1
    n_hi = BAND_QBLK + lead - REL_CLIP
    g = jnp.concatenate([jnp.repeat(rel_bias[:, -1:], n_hi, axis=1), rel_bias[:, ::-1],
                         jnp.repeat(rel_bias[:, :1], span - n_hi - n_tab, axis=1)], axis=1)
    return g.reshape(-1, 1, span).astype(F32)


def _band_attention(qkv, rel_bias):
    b, s, _ = qkv.shape
    np_ = N_HEAD_PAIRS
    g = _band_bias_vector(rel_bias)
    return pl.pallas_call(
        _band_kernel,
        grid=(np_, b),
        in_specs=[
            pl.BlockSpec((1, s, LANES), lambda p, bi: (bi, 0, p)),
            pl.BlockSpec((1, s, LANES), lambda p, bi: (bi, 0, np_ + p)),
            pl.BlockSpec((1, s, LANES), lambda p, bi: (bi, 0, 2 * np_ + p)),
            pl.BlockSpec((HEADS_PER_LANE_TILE, 1, BAND_TBL + BAND_QBLK), lambda p, bi: (p, 0, 0)),
        ],
        out_specs=pl.BlockSpec((1, s, LANES), lambda p, bi: (bi, 0, p)),
        out_shape=jax.ShapeDtypeStruct((b, s, D_MODEL), BF16),
        scratch_shapes=[
            pltpu.VMEM((HEADS_PER_LANE_TILE, BAND_QBLK, BAND_TBL), F32),
            pltpu.VMEM(((BAND_CHUNKS * CHUNK // BAND_QBLK + 1) * HEADS_PER_LANE_TILE,
                        BAND_QBLK, BAND_WIN), F32),
        ],
        compiler_params=_cparams("parallel", "arbitrary"),
        name="band_attention",
    )(qkv, qkv, qkv, g)


def kernel(x, c, ada_w, ada_b, norm_g, ffn_w_in, ffn_w_out, sb_w_qkv, sb_w_o, sg_w_in, sg_ln_g, sg_w_s, sg_bias, sg_w_out, sc_w_in, sc_conv_w, sc_w_out, cb_w_qkv, cb_rel_bias, cb_w_o):
    assert x.shape[1] % ROW_TILE == 0 and x.shape[2] == D_MODEL
    mod_all = _ada_modulation(c, ada_w, ada_b)
    for i in range(DEPTH):
        m, r = i % N_MIXERS, i // N_MIXERS
        mod = mod_all[i]
        g_mix = norm_g[i, 0:2]
        if m == 0:
            qkv = _norm_mod_proj(x, mod, norm_g[i, 0], sb_w_qkv[r], 0)
            o = _stick_breaking_attention(qkv)
            x = _proj_norm_residual(o, x, mod, norm_g[i, 1], sb_w_o[r], 2)
        elif m == 1:
            x = _sgu_sublayer(x, mod, g_mix, sg_w_in[r], sg_ln_g[r], sg_w_s[r], sg_bias[r],
                              sg_w_out[r])
        elif m == 2:
            x = _conv_sublayer(x, mod, g_mix, sc_w_in[r], sc_conv_w[r], sc_w_out[r])
        else:
            qkv = _norm_mod_proj(x, mod, norm_g[i, 0], cb_w_qkv[r], 0)
            o = _band_attention(qkv, cb_rel_bias[r])
            x = _proj_norm_residual(o, x, mod, norm_g[i, 1], cb_w_o[r], 2)
        x = _ffn_sublayer(x, mod, norm_g[i, 2:4], ffn_w_in, ffn_w_out, i)
    return x
```

```python
import functools
import math

import jax
import jax.numpy as jnp
from jax import lax
from jax.experimental import pallas as pl
from jax.experimental.pallas import tpu as pltpu

D_MODEL = 1024
DEPTH = 4
N_MIXERS = 4
N_HEADS = 16
HEAD_DIM = D_MODEL // N_HEADS
CHUNK = 64
GMLP_BLOCK = 128
GMLP_WIDTH = 2 * D_MODEL
GMLP_GROUPS = 8
GMLP_GROUP_CH = GMLP_WIDTH // GMLP_GROUPS
CONV_WIDTH = 3
BAND_CHUNKS = 8
REL_CLIP = 128
D_FF = 2816
EPS = 1e-6

LANES = 128
HEADS_PER_LANE_TILE = LANES // HEAD_DIM
N_HEAD_PAIRS = N_HEADS // HEADS_PER_LANE_TILE
VMEM_LIMIT = 56 * 1024 * 1024

ROW_TILE = 512
FFN_CHUNK = 512
SB_QBLK = 128
SB_KBLK = 256
SB_GROUPS_PER_STEP = 8
SB_EXP_ZERO = 105.0
SB_BOUND_SLACK = 1.01
BAND_QBLK = 4 * CHUNK
BAND_GROUP = 14
BAND_WIN = BAND_CHUNKS * CHUNK + BAND_QBLK
BAND_TBL = BAND_WIN + BAND_CHUNKS * CHUNK
NEG_BIG = -1e30

BF16 = jnp.bfloat16
F32 = jnp.float32


def _cparams(*sem):
    return pltpu.CompilerParams(dimension_semantics=sem, vmem_limit_bytes=VMEM_LIMIT)


def _resident(shape):
    zeros = (0,) * len(shape)
    return pl.BlockSpec(shape, lambda *_: zeros, pipeline_mode=pl.Buffered(1))


def _resident_layer(shape, layer):
    index = (layer,) + (0,) * len(shape)
    return pl.BlockSpec((pl.Squeezed(),) + tuple(shape), lambda *_: index,
                        pipeline_mode=pl.Buffered(1))


def _rms_scale(v):
    return lax.rsqrt(jnp.mean(v * v, axis=-1, keepdims=True) + EPS)


def _prologue(x, g, shift, scale):
    y = x * _rms_scale(x) * g
    return (y * (1.0 + scale) + shift).astype(BF16)


def _epilogue(x, y, g, gate):
    return x + gate * (y * _rms_scale(y) * g)


def _ada_kernel(c_ref, w_ref, b_ref, o_ref):
    c = c_ref[...]
    a = (c * jax.nn.sigmoid(c)).astype(BF16)
    w = w_ref[0].astype(BF16)
    o_ref[0] = jnp.dot(a, w, preferred_element_type=F32) + b_ref[0]


def _ada_modulation(c, ada_w, ada_b):
    b = c.shape[0]
    rows = 8
    n = ada_w.shape[-1]
    tn = 1536
    c_pad = jnp.zeros((rows, D_MODEL), F32).at[:b].set(c)
    out = pl.pallas_call(
        _ada_kernel,
        grid=(DEPTH, n // tn),
        in_specs=[
            pl.BlockSpec((rows, D_MODEL), lambda l, j: (0, 0)),
            pl.BlockSpec((1, D_MODEL, tn), lambda l, j: (l, 0, j)),
            pl.BlockSpec((1, 1, tn), lambda l, j: (l, 0, j)),
        ],
        out_specs=pl.BlockSpec((1, rows, tn), lambda l, j: (l, 0, j)),
        out_shape=jax.ShapeDtypeStruct((DEPTH, rows, n), F32),
        compiler_params=_cparams("parallel", "parallel"),
        name="ada_modulation",
    )(c_pad, ada_w, ada_b.reshape(DEPTH, 1, n))
    return out[:, :b].reshape(DEPTH, b, 6, D_MODEL)


def _proj_kernel(x_ref, mod_ref, g_ref, w_ref, o_ref, *, mod_row, n_chunk):
    h = _prologue(x_ref[0], g_ref[...], mod_ref[0, mod_row:mod_row + 1, :],
                  mod_ref[0, mod_row + 1:mod_row + 2, :])
    n = w_ref.shape[1]
    for c in range(n // n_chunk):
        sl = slice(c * n_chunk, (c + 1) * n_chunk)
        o_ref[0, :, sl] = jnp.dot(h, w_ref[:, sl].astype(BF16),
                                  preferred_element_type=F32).astype(BF16)


def _norm_mod_proj(x, mod, g, w, mod_row):
    b, s, d = x.shape
    n = w.shape[1]
    return pl.pallas_call(
        functools.partial(_proj_kernel, mod_row=mod_row, n_chunk=512),
        grid=(b, s // ROW_TILE),
        in_specs=[
            pl.BlockSpec((1, ROW_TILE, d), lambda bi, i: (bi, i, 0)),
            pl.BlockSpec((1, 6, d), lambda bi, i: (bi, 0, 0)),
            _resident((1, d)),
            _resident((d, n)),
        ],
        out_specs=pl.BlockSpec((1, ROW_TILE, n), lambda bi, i: (bi, i, 0)),
        out_shape=jax.ShapeDtypeStruct((b, s, n), BF16),
        compiler_params=_cparams("parallel", "parallel"),
        name="norm_mod_proj",
    )(x, mod, g.reshape(1, d), w)


def _out_kernel(a_ref, x_ref, mod_ref, g_ref, w_ref, o_ref, *, gate_row):
    y = jnp.dot(a_ref[0], w_ref[...].astype(BF16), preferred_element_type=F32)
    o_ref[0] = _epilogue(x_ref[0], y, g_ref[...], mod_ref[0, gate_row:gate_row + 1, :])


def _proj_norm_residual(a, x, mod, g, w, gate_row):
    b, s, d = x.shape
    k = a.shape[-1]
    return pl.pallas_call(
        functools.partial(_out_kernel, gate_row=gate_row),
        grid=(b, s // ROW_TILE),
        in_specs=[
            pl.BlockSpec((1, ROW_TILE, k), lambda bi, i: (bi, i, 0)),
            pl.BlockSpec((1, ROW_TILE, d), lambda bi, i: (bi, i, 0)),
            pl.BlockSpec((1, 6, d), lambda bi, i: (bi, 0, 0)),
            _resident((1, d)),
            _resident((k, d)),
        ],
        out_specs=pl.BlockSpec((1, ROW_TILE, d), lambda bi, i: (bi, i, 0)),
        out_shape=jax.ShapeDtypeStruct((b, s, d), F32),
        compiler_params=_cparams("parallel", "parallel"),
        name="proj_norm_residual",
    )(a, x, mod, g.reshape(1, d), w)


def _ffn_kernel(x_ref, mod_ref, g_ref, win_ref, wout_ref, o_ref, acc_ref):
    x = x_ref[0]
    h = _prologue(x, g_ref[0:1, :], mod_ref[0, 3:4, :], mod_ref[0, 4:5, :])
    for c0 in range(0, D_FF, FFN_CHUNK):
        w = min(FFN_CHUNK, D_FF - c0)
        gate = jnp.dot(h, win_ref[:, c0:c0 + w].astype(BF16), preferred_element_type=F32)
        up = jnp.dot(h, win_ref[:, D_FF + c0:D_FF + c0 + w].astype(BF16),
                     preferred_element_type=F32)
        a = (gate * jax.nn.sigmoid(gate) * up).astype(BF16)
        part = jnp.dot(a, wout_ref[c0:c0 + w, :].astype(BF16), preferred_element_type=F32)
        if c0 == 0:
            acc_ref[...] = part
        else:
            acc_ref[...] += part
    o_ref[0] = _epilogue(x, acc_ref[...], g_ref[1:2, :], mod_ref[0, 5:6, :])


def _ffn_sublayer(x, mod, g2, w_in, w_out, layer):
    b, s, d = x.shape
    return pl.pallas_call(
        _ffn_kernel,
        grid=(b, s // ROW_TILE),
        in_specs=[
            pl.BlockSpec((1, ROW_TILE, d), lambda bi, i: (bi, i, 0)),
            pl.BlockSpec((1, 6, d), lambda bi, i: (bi, 0, 0)),
            _resident((2, d)),
            _resident_layer((d, 2 * D_FF), layer),
            _resident_layer((D_FF, d), layer),
        ],
        out_specs=pl.BlockSpec((1, ROW_TILE, d), lambda bi, i: (bi, i, 0)),
        out_shape=jax.ShapeDtypeStruct((b, s, d), F32),
        scratch_shapes=[pltpu.VMEM((ROW_TILE, d), F32)],
        compiler_params=_cparams("parallel", "parallel"),
        name="ffn_sublayer",
    )(x, mod, g2, w_in, w_out)


def _gelu_tanh(x):
    c = 0.7978845608028654
    return 0.5 * x * (1.0 + jnp.tanh(c * (x + 0.044715 * (x * x * x))))


def _sgu_kernel(x_ref, mod_ref, g_ref, win_ref, lng_ref, ws_ref, sb_ref, wout_ref,
                o_ref, v_ref, y_ref):
    x = x_ref[0]
    h = _prologue(x, g_ref[0:1, :], mod_ref[0, 0:1, :], mod_ref[0, 1:2, :])
    n_chunk = 512
    half = GMLP_WIDTH // n_chunk
    for c in range(half):
        sl = slice(GMLP_WIDTH + c * n_chunk, GMLP_WIDTH + (c + 1) * n_chunk)
        v_ref[:, c * n_chunk:(c + 1) * n_chunk] = _gelu_tanh(
            jnp.dot(h, win_ref[:, sl].astype(BF16), preferred_element_type=F32))
    v = v_ref[...]
    mu = jnp.mean(v, axis=-1, keepdims=True)
    vc = v - mu
    var = jnp.mean(vc * vc, axis=-1, keepdims=True)
    vn = (vc * lax.rsqrt(var + EPS) * lng_ref[...]).astype(BF16)
    r = lax.broadcasted_iota(jnp.int32, (GMLP_BLOCK, GMLP_BLOCK), 0) // CHUNK
    cidx = lax.broadcasted_iota(jnp.int32, (GMLP_BLOCK, GMLP_BLOCK), 1) // CHUNK
    keep = r >= cidx
    for gi in range(GMLP_GROUPS):
        cs = slice(gi * GMLP_GROUP_CH, (gi + 1) * GMLP_GROUP_CH)
        u = _gelu_tanh(jnp.dot(h, win_ref[:, cs].astype(BF16), preferred_element_type=F32))
        ws = jnp.where(keep, ws_ref[gi], 0.0).astype(BF16)
        bias = sb_ref[:, gi:gi + 1]
        for blk in range(ROW_TILE // GMLP_BLOCK):
            rs = slice(blk * GMLP_BLOCK, (blk + 1) * GMLP_BLOCK)
            sv = jnp.dot(ws, vn[rs, cs], preferred_element_type=F32) + bias
            y_ref[rs, cs] = (u[rs] * sv).astype(BF16)
    y = jnp.dot(y_ref[...], wout_ref[...].astype(BF16), preferred_element_type=F32)
    o_ref[0] = _epilogue(x, y, g_ref[1:2, :], mod_ref[0, 2:3, :])


def _sgu_sublayer(x, mod, g2, w_in, ln_g, w_s, s_bias, w_out):
    b, s, d = x.shape
    return pl.pallas_call(
        _sgu_kernel,
        grid=(b, s // ROW_TILE),
        in_specs=[
            pl.BlockSpec((1, ROW_TILE, d), lambda bi, i: (bi, i, 0)),
            pl.BlockSpec((1, 6, d), lambda bi, i: (bi, 0, 0)),
            _resident((2, d)),
            _resident((d, 2 * GMLP_WIDTH)),
            _resident((1, GMLP_WIDTH)),
            _resident((GMLP_GROUPS, GMLP_BLOCK, GMLP_BLOCK)),
            _resident((GMLP_BLOCK, GMLP_GROUPS)),
            _resident((GMLP_WIDTH, d)),
        ],
        out_specs=pl.BlockSpec((1, ROW_TILE, d), lambda bi, i: (bi, i, 0)),
        out_shape=jax.ShapeDtypeStruct((b, s, d), F32),
        scratch_shapes=[pltpu.VMEM((ROW_TILE, GMLP_WIDTH), F32),
                        pltpu.VMEM((ROW_TILE, GMLP_WIDTH), BF16)],
        compiler_params=_cparams("parallel", "parallel"),
        name="sgu_sublayer",
    )(x, mod, g2, w_in, ln_g.reshape(1, GMLP_WIDTH), w_s, s_bias.T, w_out)


CONV_HALO = 8


def _conv_kernel(x_ref, mod_ref, g_ref, win_ref, cw_ref, wout_ref, o_ref, ext_ref, t_ref):
    i = pl.program_id(1)
    x = x_ref[0]
    h = _prologue(x, g_ref[0:1, :], mod_ref[0, 0:1, :], mod_ref[0, 1:2, :])

    @pl.when(i == 0)
    def _():
        ext_ref[0:CONV_HALO, :] = jnp.zeros((CONV_HALO, D_MODEL), F32)

    @pl.when(i != 0)
    def _():
        ext_ref[0:CONV_HALO, :] = ext_ref[ROW_TILE:ROW_TILE + CONV_HALO, :]

    n_chunk = 512
    for c in range(D_MODEL // n_chunk):
        cs = slice(c * n_chunk, (c + 1) * n_chunk)
        gate_c = jnp.dot(h, win_ref[:, D_MODEL + c * n_chunk:D_MODEL + (c + 1) * n_chunk]
                         .astype(BF16), preferred_element_type=F32)
        xt = jnp.dot(h, win_ref[:, 2 * D_MODEL + c * n_chunk:2 * D_MODEL + (c + 1) * n_chunk]
                     .astype(BF16), preferred_element_type=F32)
        ext_ref[CONV_HALO:, cs] = gate_c * xt
    for c in range(D_MODEL // n_chunk):
        cs = slice(c * n_chunk, (c + 1) * n_chunk)
        gate_b = jnp.dot(h, win_ref[:, cs].astype(BF16), preferred_element_type=F32)
        yc = (cw_ref[0:1, cs] * ext_ref[CONV_HALO - 2:CONV_HALO - 2 + ROW_TILE, cs]
              + cw_ref[1:2, cs] * ext_ref[CONV_HALO - 1:CONV_HALO - 1 + ROW_TILE, cs]
              + cw_ref[2:3, cs] * ext_ref[CONV_HALO:, cs])
        t_ref[:, cs] = (gate_b * yc).astype(BF16)
    y = jnp.dot(t_ref[...], wout_ref[...].astype(BF16), preferred_element_type=F32)
    o_ref[0] = _epilogue(x, y, g_ref[1:2, :], mod_ref[0, 2:3, :])


def _conv_sublayer(x, mod, g2, w_in, conv_w, w_out):
    b, s, d = x.shape
    return pl.pallas_call(
        _conv_kernel,
        grid=(b, s // ROW_TILE),
        in_specs=[
            pl.BlockSpec((1, ROW_TILE, d), lambda bi, i: (bi, i, 0)),
            pl.BlockSpec((1, 6, d), lambda bi, i: (bi, 0, 0)),
            _resident((2, d)),
            _resident((d, 3 * d)),
            _resident((CONV_WIDTH, d)),
            _resident((d, d)),
        ],
        out_specs=pl.BlockSpec((1, ROW_TILE, d), lambda bi, i: (bi, i, 0)),
        out_shape=jax.ShapeDtypeStruct((b, s, d), F32),
        scratch_shapes=[pltpu.VMEM((ROW_TILE + CONV_HALO, d), F32),
                        pltpu.VMEM((ROW_TILE, d), BF16)],
        compiler_params=_cparams("parallel", "arbitrary"),
        name="conv_sublayer",
    )(x, mod, g2, w_in, conv_w, w_out)


def _sb_kernel(q_ref, k_ref, v_ref, o_ref, qm_ref, acc_ref, run_ref, zmax_ref):
    s = q_ref.shape[1]
    nh = HEADS_PER_LANE_TILE
    n_grp = s // SB_KBLK
    rows_per_grp = SB_KBLK * nh
    chunks_per_grp = SB_KBLK // SB_QBLK
    n_super = n_grp // SB_GROUPS_PER_STEP
    lane = lax.broadcasted_iota(jnp.int32, (1, LANES), 1)
    head_masks = [(lane // HEAD_DIM == hh) for hh in range(nh)]
    tri = jnp.where(lax.broadcasted_iota(jnp.int32, (SB_KBLK, SB_KBLK), 0)
                    >= lax.broadcasted_iota(jnp.int32, (SB_KBLK, SB_KBLK), 1),
                    1.0, 0.0).astype(BF16)
    row = lax.broadcasted_iota(jnp.int32, (SB_QBLK, SB_KBLK), 0)
    col = lax.broadcasted_iota(jnp.int32, (SB_QBLK, SB_KBLK), 1)
    diag_masks = [col < row + c * SB_QBLK for c in range(chunks_per_grp)]
    scale = HEAD_DIM ** -0.5
    big = jnp.full((8, LANES), jnp.inf, F32)

    kf = k_ref[0].astype(F32)
    k_norm_max = jnp.sqrt(jnp.max(jnp.sum(kf * kf, axis=-1, keepdims=True),
                                  axis=0, keepdims=True))

    def group_rows(m):
        return pl.ds(pl.multiple_of(m * rows_per_grp, rows_per_grp), rows_per_grp)

    def setup(m):
        for c in range(chunks_per_grp):
            q0 = pl.multiple_of(m * SB_KBLK + c * SB_QBLK, SB_QBLK)
            q = q_ref[0, pl.ds(q0, SB_QBLK), :] * scale
            for hh in range(nh):
                r0 = pl.multiple_of(m * rows_per_grp + (c * nh + hh) * SB_QBLK, SB_QBLK)
                qh = jnp.where(head_masks[hh], q, jnp.zeros_like(q))
                qm_ref[pl.ds(r0, SB_QBLK), :] = qh
                qf = qh.astype(F32)
                q_norm = jnp.sqrt(jnp.sum(qf * qf, axis=-1, keepdims=True))
                zmax_ref[pl.ds(r0, SB_QBLK), :] = jnp.broadcast_to(
                    q_norm * k_norm_max * SB_BOUND_SLACK, (SB_QBLK, LANES))
        acc_ref[group_rows(m), :] = jnp.zeros((rows_per_grp, LANES), F32)
        run_ref[group_rows(m), :] = jnp.zeros((rows_per_grp, LANES), F32)

    def group_step(m, kblk, masks, active, more_keys):
        rs = group_rows(m)
        k0 = pl.multiple_of(kblk * SB_KBLK, SB_KBLK)
        kb = k_ref[0, pl.ds(k0, SB_KBLK), :]
        vb = v_ref[0, pl.ds(k0, SB_KBLK), :]
        z = lax.dot_general(qm_ref[rs, :], kb, (((1,), (1,)), ((), ())),
                            preferred_element_type=F32)
        chains = [(c, hh) for c in range(chunks_per_grp) for hh in range(nh)]
        rows = [slice(n * SB_QBLK, (n + 1) * SB_QBLK) for n in range(len(chains))]
        sps = []
        for (c, _), r in zip(chains, rows):
            zb = z[r].astype(BF16)
            sp = jnp.maximum(zb, 0.0) + jnp.log(1.0 + jnp.exp(-jnp.abs(zb)))
            if masks is not None:
                sp = jnp.where(masks[c], sp, jnp.zeros_like(sp))
            sps.append(sp)
        rest = jnp.dot(jnp.concatenate(sps, axis=0), tri, preferred_element_type=F32)
        run = run_ref[rs, :]
        a_s = []
        for (c, _), r in zip(chains, rows):
            run_c = jnp.concatenate([run[r]] * (SB_KBLK // LANES), axis=1)
            a = jnp.exp(z[r] - rest[r] - run_c)
            if masks is not None:
                a = jnp.where(masks[c], a, 0.0)
            a_s.append(a.astype(BF16))
        part = jnp.dot(jnp.concatenate(a_s, axis=0), vb, preferred_element_type=F32)
        total = jnp.broadcast_to(rest[:, 0:1], run.shape)
        if active is not None:
            part = jnp.where(active, part, 0.0)
            total = jnp.where(active, total, 0.0)
        acc_ref[rs, :] += part
        run = run + total
        run_ref[rs, :] = run
        gap = jnp.min((run - zmax_ref[rs, :]).reshape(-1, 8, LANES), axis=0)
        return gap if more_keys is None else jnp.where(more_keys, gap, big)

    def diag_step(t, gap):
        for u in range(SB_GROUPS_PER_STEP):
            m = t * SB_GROUPS_PER_STEP + u
            setup(m)
            gap = jnp.minimum(gap, group_step(m, m, diag_masks, None, m >= 1))
        return gap

    def far_step(d):
        def body(t, gap):
            for u in range(SB_GROUPS_PER_STEP):
                m = t * SB_GROUPS_PER_STEP + u
                gap = jnp.minimum(
                    gap, group_step(m, jnp.maximum(m - d, 0), None, m >= d, m >= d + 1))
            return gap
        return lax.fori_loop(d // SB_GROUPS_PER_STEP, n_super, body, big)

    def margin(gap):
        return jnp.min(gap, axis=0, keepdims=True)[0, 0]

    gap0 = lax.fori_loop(0, n_super, diag_step, big)

    def more_keys_matter(carry):
        d, mg = carry
        return (d < n_grp) & (mg < SB_EXP_ZERO)

    def walk(carry):
        d, _ = carry
        return d + 1, margin(far_step(d))

    lax.while_loop(more_keys_matter, walk, (jnp.int32(1), margin(gap0)))

    def write_out(m, _):
        for c in range(chunks_per_grp):
            r0 = pl.multiple_of(m * rows_per_grp + c * nh * SB_QBLK, SB_QBLK)
            out = acc_ref[pl.ds(r0, SB_QBLK), :]
            for hh in range(1, nh):
                out = jnp.where(head_masks[hh],
                                acc_ref[pl.ds(r0 + hh * SB_QBLK, SB_QBLK), :], out)
            q0 = pl.multiple_of(m * SB_KBLK + c * SB_QBLK, SB_QBLK)
            o_ref[0, pl.ds(q0, SB_QBLK), :] = out.astype(BF16)
        return 0

    lax.fori_loop(0, n_grp, write_out, 0)


def _stick_breaking_attention(qkv):
    b, s, _ = qkv.shape
    np_ = N_HEAD_PAIRS
    assert s % (SB_KBLK * SB_GROUPS_PER_STEP) == 0
    rows = s * HEADS_PER_LANE_TILE
    return pl.pallas_call(
        _sb_kernel,
        grid=(b, np_),
        in_specs=[
            pl.BlockSpec((1, s, LANES), lambda bi, p: (bi, 0, p)),
            pl.BlockSpec((1, s, LANES), lambda bi, p: (bi, 0, np_ + p)),
            pl.BlockSpec((1, s, LANES), lambda bi, p: (bi, 0, 2 * np_ + p)),
        ],
        out_specs=pl.BlockSpec((1, s, LANES), lambda bi, p: (bi, 0, p)),
        out_shape=jax.ShapeDtypeStruct((b, s, D_MODEL), BF16),
        scratch_shapes=[
            pltpu.VMEM((rows, LANES), BF16),
            pltpu.VMEM((rows, LANES), F32),
            pltpu.VMEM((rows, LANES), F32),
            pltpu.VMEM((rows, LANES), F32),
        ],
        compiler_params=_cparams("parallel", "parallel"),
        name="stick_breaking_attention",
    )(qkv, qkv, qkv)


def _band_kernel(q_ref, k_ref, v_ref, g_ref, o_ref, tbl_ref, bias_ref):
    s = q_ref.shape[1]
    lead = BAND_CHUNKS * CHUNK
    nh = HEADS_PER_LANE_TILE
    n_head = lead // BAND_QBLK
    chunks_per_blk = BAND_QBLK // CHUNK
    lane = lax.broadcasted_iota(jnp.int32, (1, LANES), 1)
    head_masks = [(lane // HEAD_DIM == hh) for hh in range(nh)]
    scale = HEAD_DIM ** -0.5

    @pl.when(pl.program_id(1) == 0)
    def _():
        for hh in range(nh):
            g_rows = jnp.broadcast_to(g_ref[hh], (BAND_QBLK, BAND_TBL + BAND_QBLK))
            tbl_ref[hh] = pltpu.roll(g_rows, 0, 1, stride=1, stride_axis=0)[:, BAND_QBLK:]
        row_chunk = lax.broadcasted_iota(jnp.int32, (BAND_QBLK, BAND_WIN), 0) // CHUNK
        col_chunk = lax.broadcasted_iota(jnp.int32, (BAND_QBLK, BAND_WIN), 1) // CHUNK
        for slot in range(n_head + 1):
            first_key_chunk_minus_query_chunk = -min(slot * chunks_per_blk, BAND_CHUNKS)
            tbl_off = lead + first_key_chunk_minus_query_chunk * CHUNK
            rel = col_chunk + first_key_chunk_minus_query_chunk - row_chunk
            valid = (rel <= 0) & (rel >= -BAND_CHUNKS)
            for hh in range(nh):
                bias_ref[slot * nh + hh] = jnp.where(
                    valid, tbl_ref[hh, :, tbl_off:tbl_off + BAND_WIN], NEG_BIG)

    def q_block(q0, k0, slot):
        q = q_ref[0, pl.ds(q0, BAND_QBLK), :] * scale
        zero = jnp.zeros_like(q)
        kb = k_ref[0, pl.ds(k0, BAND_WIN), :]
        vb = v_ref[0, pl.ds(k0, BAND_WIN), :]
        qh = jnp.concatenate([jnp.where(head_masks[hh], q, zero) for hh in range(nh)], axis=0)
        sc = lax.dot_general(qh, kb, (((1,), (1,)), ((), ())), preferred_element_type=F32)
        sc = sc + jnp.concatenate([bias_ref[slot * nh + hh] for hh in range(nh)], axis=0)
        m = jnp.max(sc, axis=-1, keepdims=True)
        e = jnp.exp(sc - m)
        denom = jnp.sum(e, axis=-1, keepdims=True)
        oh = jnp.dot(e.astype(BF16), vb, preferred_element_type=F32) / denom
        out = oh[0:BAND_QBLK]
        for hh in range(1, nh):
            out = jnp.where(head_masks[hh], oh[hh * BAND_QBLK:(hh + 1) * BAND_QBLK], out)
        o_ref[0, pl.ds(q0, BAND_QBLK), :] = out.astype(BF16)

    for i in range(n_head):
        q_block(i * BAND_QBLK, 0, i)

    n_general = s // BAND_QBLK - n_head
    group = math.gcd(n_general, BAND_GROUP)

    def body(t, _):
        for u in range(group):
            q0 = pl.multiple_of((n_head + t * group + u) * BAND_QBLK, BAND_QBLK)
            q_block(q0, pl.multiple_of(q0 - lead, BAND_QBLK), n_head)
        return 0

    lax.fori_loop(0, n_general // group, body, 0)


def _band_bias_vector(rel_bias):
    lead = BAND_CHUNKS * CHUNK
    span = BAND_TBL + BAND_QBLK
    n_tab = 2 * REL_CLIP + 1
    n_hi = BAND_QBLK + lead - REL_CLIP
    g = jnp.concatenate([jnp.repeat(rel_bias[:, -1:], n_hi, axis=1), rel_bias[:, ::-1],
                         jnp.repeat(rel_bias[:, :1], span - n_hi - n_tab, axis=1)], axis=1)
    return g.reshape(-1, 1, span).astype(F32)


def _band_attention(qkv, rel_bias):
    b, s, _ = qkv.shape
    np_ = N_HEAD_PAIRS
    g = _band_bias_vector(rel_bias)
    return pl.pallas_call(
        _band_kernel,
        grid=(np_, b),
        in_specs=[
            pl.BlockSpec((1, s, LANES), lambda p, bi: (bi, 0, p)),
            pl.BlockSpec((1, s, LANES), lambda p, bi: (bi, 0, np_ + p)),
            pl.BlockSpec((1, s, LANES), lambda p, bi: (bi, 0, 2 * np_ + p)),
            pl.BlockSpec((HEADS_PER_LANE_TILE, 1, BAND_TBL + BAND_QBLK), lambda p, bi: (p, 0, 0)),
        ],
        out_specs=pl.BlockSpec((1, s, LANES), lambda p, bi: (bi, 0, p)),
        out_shape=jax.ShapeDtypeStruct((b, s, D_MODEL), BF16),
        scratch_shapes=[
            pltpu.VMEM((HEADS_PER_LANE_TILE, BAND_QBLK, BAND_TBL), F32),
            pltpu.VMEM(((BAND_CHUNKS * CHUNK // BAND_QBLK + 1) * HEADS_PER_LANE_TILE,
                        BAND_QBLK, BAND_WIN), F32),
        ],
        compiler_params=_cparams("parallel", "arbitrary"),
        name="band_attention",
    )(qkv, qkv, qkv, g)


def kernel(x, c, ada_w, ada_b, norm_g, ffn_w_in, ffn_w_out, sb_w_qkv, sb_w_o, sg_w_in, sg_ln_g, sg_w_s, sg_bias, sg_w_out, sc_w_in, sc_conv_w, sc_w_out, cb_w_qkv, cb_rel_bias, cb_w_o):
    assert x.shape[1] % ROW_TILE == 0 and x.shape[2] == D_MODEL
    mod_all = _ada_modulation(c, ada_w, ada_b)
    for i in range(DEPTH):
        m, r = i % N_MIXERS, i // N_MIXERS
        mod = mod_all[i]
        g_mix = norm_g[i, 0:2]
        if m == 0:
            qkv = _norm_mod_proj(x, mod, norm_g[i, 0], sb_w_qkv[r], 0)
            o = _stick_breaking_attention(qkv)
            x = _proj_norm_residual(o, x, mod, norm_g[i, 1], sb_w_o[r], 2)
        elif m == 1:
            x = _sgu_sublayer(x, mod, g_mix, sg_w_in[r], sg_ln_g[r], sg_w_s[r], sg_bias[r],
                              sg_w_out[r])
        elif m == 2:
            x = _conv_sublayer(x, mod, g_mix, sc_w_in[r], sc_conv_w[r], sc_w_out[r])
        else:
            qkv = _norm_mod_proj(x, mod, norm_g[i, 0], cb_w_qkv[r], 0)
            o = _band_attention(qkv, cb_rel_bias[r])
            x = _proj_norm_residual(o, x, mod, norm_g[i, 1], cb_w_o[r], 2)
        x = _ffn_sublayer(x, mod, norm_g[i, 2:4], ffn_w_in, ffn_w_out, i)
    return x
```

```python
import functools
import math

import jax
import jax.numpy as jnp
from jax import lax
from jax.experimental import pallas as pl
from jax.experimental.pallas import tpu as pltpu

D_MODEL = 1024
DEPTH = 4
N_MIXERS = 4
N_HEADS = 16
HEAD_DIM = D_MODEL // N_HEADS
CHUNK = 64
GMLP_BLOCK = 128
GMLP_WIDTH = 2 * D_MODEL
GMLP_GROUPS = 8
GMLP_GROUP_CH = GMLP_WIDTH // GMLP_GROUPS
CONV_WIDTH = 3
BAND_CHUNKS = 8
REL_CLIP = 128
D_FF = 2816
EPS = 1e-6

LANES = 128
HEADS_PER_LANE_TILE = LANES // HEAD_DIM
N_HEAD_PAIRS = N_HEADS // HEADS_PER_LANE_TILE
VMEM_LIMIT = 56 * 1024 * 1024
FFN_VMEM_LIMIT = 60 * 1024 * 1024

ROW_TILE = 512
FFN_CHUNK = 512
SB_QBLK = 128
SB_KBLK = 256
SB_GROUPS_PER_STEP = 16
SB_EXP_ZERO = 105.0
SB_BOUND_SLACK = 1.01
BAND_QBLK = 4 * CHUNK
BAND_GROUP = 14
BAND_WIN = BAND_CHUNKS * CHUNK + BAND_QBLK
BAND_TBL = BAND_WIN + BAND_CHUNKS * CHUNK
NEG_BIG = -1e30

BF16 = jnp.bfloat16
F32 = jnp.float32


def _cparams(*sem):
    return pltpu.CompilerParams(dimension_semantics=sem, vmem_limit_bytes=VMEM_LIMIT)


def _resident(shape):
    zeros = (0,) * len(shape)
    return pl.BlockSpec(shape, lambda *_: zeros, pipeline_mode=pl.Buffered(1))


def _resident_layer(shape, layer):
    index = (layer,) + (0,) * len(shape)
    return pl.BlockSpec((pl.Squeezed(),) + tuple(shape), lambda *_: index,
                        pipeline_mode=pl.Buffered(1))


def _rms_scale(v):
    return lax.rsqrt(jnp.mean(v * v, axis=-1, keepdims=True) + EPS)


def _prologue(x, g, shift, scale):
    y = x * _rms_scale(x) * g
    return (y * (1.0 + scale) + shift).astype(BF16)


def _epilogue(x, y, g, gate):
    return x + gate * (y * _rms_scale(y) * g)


def _ada_kernel(c_ref, w_ref, b_ref, o_ref):
    c = c_ref[...]
    a = (c * jax.nn.sigmoid(c)).astype(BF16)
    w = w_ref[0].astype(BF16)
    o_ref[0] = jnp.dot(a, w, preferred_element_type=F32) + b_ref[0]


def _ada_modulation(c, ada_w, ada_b):
    b = c.shape[0]
    rows = 8
    n = ada_w.shape[-1]
    tn = 1536
    c_pad = jnp.zeros((rows, D_MODEL), F32).at[:b].set(c)
    out = pl.pallas_call(
        _ada_kernel,
        grid=(DEPTH, n // tn),
        in_specs=[
            pl.BlockSpec((rows, D_MODEL), lambda l, j: (0, 0)),
            pl.BlockSpec((1, D_MODEL, tn), lambda l, j: (l, 0, j)),
            pl.BlockSpec((1, 1, tn), lambda l, j: (l, 0, j)),
        ],
        out_specs=pl.BlockSpec((1, rows, tn), lambda l, j: (l, 0, j)),
        out_shape=jax.ShapeDtypeStruct((DEPTH, rows, n), F32),
        compiler_params=_cparams("parallel", "parallel"),
        name="ada_modulation",
    )(c_pad, ada_w, ada_b.reshape(DEPTH, 1, n))
    return out[:, :b].reshape(DEPTH, b, 6, D_MODEL)


def _proj_kernel(x_ref, mod_ref, g_ref, w_ref, o_ref, *, mod_row, n_chunk):
    h = _prologue(x_ref[0], g_ref[...], mod_ref[0, mod_row:mod_row + 1, :],
                  mod_ref[0, mod_row + 1:mod_row + 2, :])
    n = w_ref.shape[1]
    for c in range(n // n_chunk):
        sl = slice(c * n_chunk, (c + 1) * n_chunk)
        o_ref[0, :, sl] = jnp.dot(h, w_ref[:, sl].astype(BF16),
                                  preferred_element_type=F32).astype(BF16)


def _norm_mod_proj(x, mod, g, w, mod_row):
    b, s, d = x.shape
    n = w.shape[1]
    return pl.pallas_call(
        functools.partial(_proj_kernel, mod_row=mod_row, n_chunk=512),
        grid=(b, s // ROW_TILE),
        in_specs=[
            pl.BlockSpec((1, ROW_TILE, d), lambda bi, i: (bi, i, 0)),
            pl.BlockSpec((1, 6, d), lambda bi, i: (bi, 0, 0)),
            _resident((1, d)),
            _resident((d, n)),
        ],
        out_specs=pl.BlockSpec((1, ROW_TILE, n), lambda bi, i: (bi, i, 0)),
        out_shape=jax.ShapeDtypeStruct((b, s, n), BF16),
        compiler_params=_cparams("parallel", "parallel"),
        name="norm_mod_proj",
    )(x, mod, g.reshape(1, d), w)


def _ffn_body(x, mod_ref, g_ref, win_ref, wout_ref, acc_ref):
    h = _prologue(x, g_ref[2:3, :], mod_ref[0, 3:4, :], mod_ref[0, 4:5, :])
    for c0 in range(0, D_FF, FFN_CHUNK):
        w = min(FFN_CHUNK, D_FF - c0)
        gate = jnp.dot(h, win_ref[:, c0:c0 + w].astype(BF16), preferred_element_type=F32)
        up = jnp.dot(h, win_ref[:, D_FF + c0:D_FF + c0 + w].astype(BF16),
                     preferred_element_type=F32)
        a = (gate * jax.nn.sigmoid(gate) * up).astype(BF16)
        part = jnp.dot(a, wout_ref[c0:c0 + w, :].astype(BF16), preferred_element_type=F32)
        if c0 == 0:
            acc_ref[...] = part
        else:
            acc_ref[...] += part
    return _epilogue(x, acc_ref[...], g_ref[3:4, :], mod_ref[0, 5:6, :])


def _ffn_kernel(x_ref, mod_ref, g_ref, win_ref, wout_ref, o_ref, acc_ref):
    o_ref[0] = _ffn_body(x_ref[0], mod_ref, g_ref, win_ref, wout_ref, acc_ref)


def _proj_ffn_kernel(a_ref, x_ref, mod_ref, g_ref, wo_ref, win_ref, wout_ref, o_ref,
                     acc_ref, mid_ref):
    y = jnp.dot(a_ref[0], wo_ref[...], preferred_element_type=F32)
    mid_ref[...] = _epilogue(x_ref[0], y, g_ref[1:2, :], mod_ref[0, 2:3, :])
    o_ref[0] = _ffn_body(mid_ref[...], mod_ref, g_ref, win_ref, wout_ref, acc_ref)


def _ffn_sublayer(x, mod, g4, w_in, w_out, layer, attn_out=None, w_o=None):
    b, s, d = x.shape
    row_spec = pl.BlockSpec((1, ROW_TILE, d), lambda bi, i: (bi, i, 0))
    common = [
        pl.BlockSpec((1, 6, d), lambda bi, i: (bi, 0, 0)),
        _resident((4, d)),
    ]
    weights = [_resident_layer((d, 2 * D_FF), layer), _resident_layer((D_FF, d), layer)]
    scratch = [pltpu.VMEM((ROW_TILE, d), F32)]
    if attn_out is None:
        body, in_specs, args = _ffn_kernel, [row_spec] + common + weights, (x, mod, g4, w_in, w_out)
    else:
        body = _proj_ffn_kernel
        in_specs = [row_spec, row_spec] + common + [_resident((d, d))] + weights
        args = (attn_out, x, mod, g4, w_o.astype(BF16), w_in, w_out)
        scratch = scratch + [pltpu.VMEM((ROW_TILE, d), F32)]
    return pl.pallas_call(
        body,
        grid=(b, s // ROW_TILE),
        in_specs=in_specs,
        out_specs=row_spec,
        out_shape=jax.ShapeDtypeStruct((b, s, d), F32),
        scratch_shapes=scratch,
        compiler_params=pltpu.CompilerParams(
            dimension_semantics=("parallel", "parallel"), vmem_limit_bytes=FFN_VMEM_LIMIT),
        name="ffn_sublayer" if attn_out is None else "proj_ffn_sublayer",
    )(*args)


def _gelu_tanh(x):
    c = 0.7978845608028654
    return 0.5 * x * (1.0 + jnp.tanh(c * (x + 0.044715 * (x * x * x))))


def _sgu_kernel(x_ref, mod_ref, g_ref, win_ref, lng_ref, ws_ref, sb_ref, wout_ref,
                o_ref, v_ref, y_ref):
    x = x_ref[0]
    h = _prologue(x, g_ref[0:1, :], mod_ref[0, 0:1, :], mod_ref[0, 1:2, :])
    n_chunk = 512
    half = GMLP_WIDTH // n_chunk
    for c in range(half):
        sl = slice(GMLP_WIDTH + c * n_chunk, GMLP_WIDTH + (c + 1) * n_chunk)
        v_ref[:, c * n_chunk:(c + 1) * n_chunk] = _gelu_tanh(
            jnp.dot(h, win_ref[:, sl].astype(BF16), preferred_element_type=F32))
    v = v_ref[...]
    mu = jnp.mean(v, axis=-1, keepdims=True)
    vc = v - mu
    var = jnp.mean(vc * vc, axis=-1, keepdims=True)
    vn = (vc * lax.rsqrt(var + EPS) * lng_ref[...]).astype(BF16)
    r = lax.broadcasted_iota(jnp.int32, (GMLP_BLOCK, GMLP_BLOCK), 0) // CHUNK
    cidx = lax.broadcasted_iota(jnp.int32, (GMLP_BLOCK, GMLP_BLOCK), 1) // CHUNK
    keep = r >= cidx
    for gi in range(GMLP_GROUPS):
        cs = slice(gi * GMLP_GROUP_CH, (gi + 1) * GMLP_GROUP_CH)
        u = _gelu_tanh(jnp.dot(h, win_ref[:, cs].astype(BF16), preferred_element_type=F32))
        ws = jnp.where(keep, ws_ref[gi], 0.0).astype(BF16)
        bias = sb_ref[:, gi:gi + 1]
        for blk in range(ROW_TILE // GMLP_BLOCK):
            rs = slice(blk * GMLP_BLOCK, (blk + 1) * GMLP_BLOCK)
            sv = jnp.dot(ws, vn[rs, cs], preferred_element_type=F32) + bias
            y_ref[rs, cs] = (u[rs] * sv).astype(BF16)
    y = jnp.dot(y_ref[...], wout_ref[...].astype(BF16), preferred_element_type=F32)
    o_ref[0] = _epilogue(x, y, g_ref[1:2, :], mod_ref[0, 2:3, :])


def _sgu_sublayer(x, mod, g2, w_in, ln_g, w_s, s_bias, w_out):
    b, s, d = x.shape
    return pl.pallas_call(
        _sgu_kernel,
        grid=(b, s // ROW_TILE),
        in_specs=[
            pl.BlockSpec((1, ROW_TILE, d), lambda bi, i: (bi, i, 0)),
            pl.BlockSpec((1, 6, d), lambda bi, i: (bi, 0, 0)),
            _resident((2, d)),
            _resident((d, 2 * GMLP_WIDTH)),
            _resident((1, GMLP_WIDTH)),
            _resident((GMLP_GROUPS, GMLP_BLOCK, GMLP_BLOCK)),
            _resident((GMLP_BLOCK, GMLP_GROUPS)),
            _resident((GMLP_WIDTH, d)),
        ],
        out_specs=pl.BlockSpec((1, ROW_TILE, d), lambda bi, i: (bi, i, 0)),
        out_shape=jax.ShapeDtypeStruct((b, s, d), F32),
        scratch_shapes=[pltpu.VMEM((ROW_TILE, GMLP_WIDTH), F32),
                        pltpu.VMEM((ROW_TILE, GMLP_WIDTH), BF16)],
        compiler_params=_cparams("parallel", "parallel"),
        name="sgu_sublayer",
    )(x, mod, g2, w_in, ln_g.reshape(1, GMLP_WIDTH), w_s, s_bias.T, w_out)


CONV_HALO = 8


def _conv_kernel(x_ref, mod_ref, g_ref, win_ref, cw_ref, wout_ref, o_ref, ext_ref, t_ref):
    i = pl.program_id(1)
    x = x_ref[0]
    h = _prologue(x, g_ref[0:1, :], mod_ref[0, 0:1, :], mod_ref[0, 1:2, :])

    @pl.when(i == 0)
    def _():
        ext_ref[0:CONV_HALO, :] = jnp.zeros((CONV_HALO, D_MODEL), F32)

    @pl.when(i != 0)
    def _():
        ext_ref[0:CONV_HALO, :] = ext_ref[ROW_TILE:ROW_TILE + CONV_HALO, :]

    n_chunk = 512
    for c in range(D_MODEL // n_chunk):
        cs = slice(c * n_chunk, (c + 1) * n_chunk)
        gate_c = jnp.dot(h, win_ref[:, D_MODEL + c * n_chunk:D_MODEL + (c + 1) * n_chunk]
                         .astype(BF16), preferred_element_type=F32)
        xt = jnp.dot(h, win_ref[:, 2 * D_MODEL + c * n_chunk:2 * D_MODEL + (c + 1) * n_chunk]
                     .astype(BF16), preferred_element_type=F32)
        ext_ref[CONV_HALO:, cs] = gate_c * xt
    for c in range(D_MODEL // n_chunk):
        cs = slice(c * n_chunk, (c + 1) * n_chunk)
        gate_b = jnp.dot(h, win_ref[:, cs].astype(BF16), preferred_element_type=F32)
        yc = (cw_ref[0:1, cs] * ext_ref[CONV_HALO - 2:CONV_HALO - 2 + ROW_TILE, cs]
              + cw_ref[1:2, cs] * ext_ref[CONV_HALO - 1:CONV_HALO - 1 + ROW_TILE, cs]
              + cw_ref[2:3, cs] * ext_ref[CONV_HALO:, cs])
        t_ref[:, cs] = (gate_b * yc).astype(BF16)
    y = jnp.dot(t_ref[...], wout_ref[...].astype(BF16), preferred_element_type=F32)
    o_ref[0] = _epilogue(x, y, g_ref[1:2, :], mod_ref[0, 2:3, :])


def _conv_sublayer(x, mod, g2, w_in, conv_w, w_out):
    b, s, d = x.shape
    return pl.pallas_call(
        _conv_kernel,
        grid=(b, s // ROW_TILE),
        in_specs=[
            pl.BlockSpec((1, ROW_TILE, d), lambda bi, i: (bi, i, 0)),
            pl.BlockSpec((1, 6, d), lambda bi, i: (bi, 0, 0)),
            _resident((2, d)),
            _resident((d, 3 * d)),
            _resident((CONV_WIDTH, d)),
            _resident((d, d)),
        ],
        out_specs=pl.BlockSpec((1, ROW_TILE, d), lambda bi, i: (bi, i, 0)),
        out_shape=jax.ShapeDtypeStruct((b, s, d), F32),
        scratch_shapes=[pltpu.VMEM((ROW_TILE + CONV_HALO, d), F32),
                        pltpu.VMEM((ROW_TILE, d), BF16)],
        compiler_params=_cparams("parallel", "arbitrary"),
        name="conv_sublayer",
    )(x, mod, g2, w_in, conv_w, w_out)


def _sb_kernel(q_ref, k_ref, v_ref, o_ref, qm_ref, acc_ref, run_ref, zmax_ref):
    s = q_ref.shape[1]
    nh = HEADS_PER_LANE_TILE
    n_grp = s // SB_KBLK
    rows_per_grp = SB_KBLK * nh
    chunks_per_grp = SB_KBLK // SB_QBLK
    n_super = n_grp // SB_GROUPS_PER_STEP
    lane = lax.broadcasted_iota(jnp.int32, (1, LANES), 1)
    head_masks = [(lane // HEAD_DIM == hh) for hh in range(nh)]
    tri = jnp.where(lax.broadcasted_iota(jnp.int32, (SB_KBLK, SB_KBLK), 0)
                    >= lax.broadcasted_iota(jnp.int32, (SB_KBLK, SB_KBLK), 1),
                    1.0, 0.0).astype(BF16)
    row = lax.broadcasted_iota(jnp.int32, (SB_QBLK, SB_KBLK), 0)
    col = lax.broadcasted_iota(jnp.int32, (SB_QBLK, SB_KBLK), 1)
    diag_masks = [col < row + c * SB_QBLK for c in range(chunks_per_grp)]
    scale = HEAD_DIM ** -0.5
    big = jnp.full((8, LANES), jnp.inf, F32)

    kf = k_ref[0].astype(F32)
    k_norm_max = jnp.sqrt(jnp.max(jnp.sum(kf * kf, axis=-1, keepdims=True),
                                  axis=0, keepdims=True))

    def group_rows(m):
        return pl.ds(pl.multiple_of(m * rows_per_grp, rows_per_grp), rows_per_grp)

    def setup(m):
        for c in range(chunks_per_grp):
            q0 = pl.multiple_of(m * SB_KBLK + c * SB_QBLK, SB_QBLK)
            q = q_ref[0, pl.ds(q0, SB_QBLK), :] * scale
            for hh in range(nh):
                r0 = pl.multiple_of(m * rows_per_grp + (c * nh + hh) * SB_QBLK, SB_QBLK)
                qh = jnp.where(head_masks[hh], q, jnp.zeros_like(q))
                qm_ref[pl.ds(r0, SB_QBLK), :] = qh
                qf = qh.astype(F32)
                q_norm = jnp.sqrt(jnp.sum(qf * qf, axis=-1, keepdims=True))
                zmax_ref[pl.ds(r0, SB_QBLK), :] = jnp.broadcast_to(
                    q_norm * k_norm_max * SB_BOUND_SLACK, (SB_QBLK, LANES))
        acc_ref[group_rows(m), :] = jnp.zeros((rows_per_grp, LANES), F32)
        run_ref[group_rows(m), :] = jnp.zeros((rows_per_grp, LANES), F32)

    def group_step(m, kblk, masks, active, more_keys):
        rs = group_rows(m)
        k0 = pl.multiple_of(kblk * SB_KBLK, SB_KBLK)
        kb = k_ref[0, pl.ds(k0, SB_KBLK), :]
        vb = v_ref[0, pl.ds(k0, SB_KBLK), :]
        z = lax.dot_general(qm_ref[rs, :], kb, (((1,), (1,)), ((), ())),
                            preferred_element_type=F32)
        chains = [(c, hh) for c in range(chunks_per_grp) for hh in range(nh)]
        rows = [slice(n * SB_QBLK, (n + 1) * SB_QBLK) for n in range(len(chains))]
        sps = []
        for (c, _), r in zip(chains, rows):
            zb = z[r].astype(BF16)
            sp = jnp.maximum(zb, 0.0) + jnp.log(1.0 + jnp.exp(-jnp.abs(zb)))
            if masks is not None:
                sp = jnp.where(masks[c], sp, jnp.zeros_like(sp))
            sps.append(sp)
        rest = jnp.dot(jnp.concatenate(sps, axis=0), tri, preferred_element_type=F32)
        run = run_ref[rs, :]
        a_s = []
        for (c, _), r in zip(chains, rows):
            run_c = jnp.concatenate([run[r]] * (SB_KBLK // LANES), axis=1)
            a = jnp.exp(z[r] - rest[r] - run_c)
            if masks is not None:
                a = jnp.where(masks[c], a, 0.0)
            a_s.append(a.astype(BF16))
        part = jnp.dot(jnp.concatenate(a_s, axis=0), vb, preferred_element_type=F32)
        total = jnp.broadcast_to(rest[:, 0:1], run.shape)
        if active is not None:
            part = jnp.where(active, part, 0.0)
            total = jnp.where(active, total, 0.0)
        acc_ref[rs, :] += part
        run = run + total
        run_ref[rs, :] = run
        gap = jnp.min((run - zmax_ref[rs, :]).reshape(-1, 8, LANES), axis=0)
        return gap if more_keys is None else jnp.where(more_keys, gap, big)

    def diag_step(t, gap):
        for u in range(SB_GROUPS_PER_STEP):
            m = t * SB_GROUPS_PER_STEP + u
            setup(m)
            gap = jnp.minimum(gap, group_step(m, m, diag_masks, None, m >= 1))
        return gap

    def far_step(d):
        def body(t, gap):
            for u in range(SB_GROUPS_PER_STEP):
                m = t * SB_GROUPS_PER_STEP + u
                gap = jnp.minimum(
                    gap, group_step(m, jnp.maximum(m - d, 0), None, m >= d, m >= d + 1))
            return gap
        return lax.fori_loop(d // SB_GROUPS_PER_STEP, n_super, body, big)

    def margin(gap):
        return jnp.min(gap, axis=0, keepdims=True)[0, 0]

    gap0 = lax.fori_loop(0, n_super, diag_step, big)

    def more_keys_matter(carry):
        d, mg = carry
        return (d < n_grp) & (mg < SB_EXP_ZERO)

    def walk(carry):
        d, _ = carry
        return d + 1, margin(far_step(d))

    lax.while_loop(more_keys_matter, walk, (jnp.int32(1), margin(gap0)))

    def write_out(m, _):
        for c in range(chunks_per_grp):
            r0 = pl.multiple_of(m * rows_per_grp + c * nh * SB_QBLK, SB_QBLK)
            out = acc_ref[pl.ds(r0, SB_QBLK), :]
            for hh in range(1, nh):
                out = jnp.where(head_masks[hh],
                                acc_ref[pl.ds(r0 + hh * SB_QBLK, SB_QBLK), :], out)
            q0 = pl.multiple_of(m * SB_KBLK + c * SB_QBLK, SB_QBLK)
            o_ref[0, pl.ds(q0, SB_QBLK), :] = out.astype(BF16)
        return 0

    lax.fori_loop(0, n_grp, write_out, 0)


def _stick_breaking_attention(qkv):
    b, s, _ = qkv.shape
    np_ = N_HEAD_PAIRS
    assert s % (SB_KBLK * SB_GROUPS_PER_STEP) == 0
    rows = s * HEADS_PER_LANE_TILE
    return pl.pallas_call(
        _sb_kernel,
        grid=(b, np_),
        in_specs=[
            pl.BlockSpec((1, s, LANES), lambda bi, p: (bi, 0, p)),
            pl.BlockSpec((1, s, LANES), lambda bi, p: (bi, 0, np_ + p)),
            pl.BlockSpec((1, s, LANES), lambda bi, p: (bi, 0, 2 * np_ + p)),
        ],
        out_specs=pl.BlockSpec((1, s, LANES), lambda bi, p: (bi, 0, p)),
        out_shape=jax.ShapeDtypeStruct((b, s, D_MODEL), BF16),
        scratch_shapes=[
            pltpu.VMEM((rows, LANES), BF16),
            pltpu.VMEM((rows, LANES), F32),
            pltpu.VMEM((rows, LANES), F32),
            pltpu.VMEM((rows, LANES), F32),
        ],
        compiler_params=_cparams("parallel", "parallel"),
        name="stick_breaking_attention",
    )(qkv, qkv, qkv)


def _band_kernel(q_ref, k_ref, v_ref, g_ref, o_ref, tbl_ref, bias_ref):
    s = q_ref.shape[1]
    lead = BAND_CHUNKS * CHUNK
    nh = HEADS_PER_LANE_TILE
    n_head = lead // BAND_QBLK
    chunks_per_blk = BAND_QBLK // CHUNK
    lane = lax.broadcasted_iota(jnp.int32, (1, LANES), 1)
    head_masks = [(lane // HEAD_DIM == hh) for hh in range(nh)]
    scale = HEAD_DIM ** -0.5

    @pl.when(pl.program_id(1) == 0)
    def _():
        for hh in range(nh):
            g_rows = jnp.broadcast_to(g_ref[hh], (BAND_QBLK, BAND_TBL + BAND_QBLK))
            tbl_ref[hh] = pltpu.roll(g_rows, 0, 1, stride=1, stride_axis=0)[:, BAND_QBLK:]
        row_chunk = lax.broadcasted_iota(jnp.int32, (BAND_QBLK, BAND_WIN), 0) // CHUNK
        col_chunk = lax.broadcasted_iota(jnp.int32, (BAND_QBLK, BAND_WIN), 1) // CHUNK
        for slot in range(n_head + 1):
            first_key_chunk_minus_query_chunk = -min(slot * chunks_per_blk, BAND_CHUNKS)
            tbl_off = lead + first_key_chunk_minus_query_chunk * CHUNK
            rel = col_chunk + first_key_chunk_minus_query_chunk - row_chunk
            valid = (rel <= 0) & (rel >= -BAND_CHUNKS)
            for hh in range(nh):
                bias_ref[slot * nh + hh] = jnp.where(
                    valid, tbl_ref[hh, :, tbl_off:tbl_off + BAND_WIN], NEG_BIG)

    def q_block(q0, k0, slot):
        q = q_ref[0, pl.ds(q0, BAND_QBLK), :] * scale
        zero = jnp.zeros_like(q)
        kb = k_ref[0, pl.ds(k0, BAND_WIN), :]
        vb = v_ref[0, pl.ds(k0, BAND_WIN), :]
        qh = jnp.concatenate([jnp.where(head_masks[hh], q, zero) for hh in range(nh)], axis=0)
        sc = lax.dot_general(qh, kb, (((1,), (1,)), ((), ())), preferred_element_type=F32)
        sc = sc + jnp.concatenate([bias_ref[slot * nh + hh] for hh in range(nh)], axis=0)
        m = jnp.max(sc, axis=-1, keepdims=True)
        e = jnp.exp(sc - m)
        denom = jnp.sum(e, axis=-1, keepdims=True)
        oh = jnp.dot(e.astype(BF16), vb, preferred_element_type=F32) / denom
        out = oh[0:BAND_QBLK]
        for hh in range(1, nh):
            out = jnp.where(head_masks[hh], oh[hh * BAND_QBLK:(hh + 1) * BAND_QBLK], out)
        o_ref[0, pl.ds(q0, BAND_QBLK), :] = out.astype(BF16)

    for i in range(n_head):
        q_block(i * BAND_QBLK, 0, i)

    n_general = s // BAND_QBLK - n_head
    group = math.gcd(n_general, BAND_GROUP)

    def body(t, _):
        for u in range(group):
            q0 = pl.multiple_of((n_head + t * group + u) * BAND_QBLK, BAND_QBLK)
            q_block(q0, pl.multiple_of(q0 - lead, BAND_QBLK), n_head)
        return 0

    lax.fori_loop(0, n_general // group, body, 0)


def _band_bias_vector(rel_bias):
    lead = BAND_CHUNKS * CHUNK
    span = BAND_TBL + BAND_QBLK
    n_tab = 2 * REL_CLIP + 1
    n_hi = BAND_QBLK + lead - REL_CLIP
    g = jnp.concatenate([jnp.repeat(rel_bias[:, -1:], n_hi, axis=1), rel_bias[:, ::-1],
                         jnp.repeat(rel_bias[:, :1], span - n_hi - n_tab, axis=1)], axis=1)
    return g.reshape(-1, 1, span).astype(F32)


def _band_attention(qkv, rel_bias):
    b, s, _ = qkv.shape
    np_ = N_HEAD_PAIRS
    g = _band_bias_vector(rel_bias)
    return pl.pallas_call(
        _band_kernel,
        grid=(np_, b),
        in_specs=[
            pl.BlockSpec((1, s, LANES), lambda p, bi: (bi, 0, p)),
            pl.BlockSpec((1, s, LANES), lambda p, bi: (bi, 0, np_ + p)),
            pl.BlockSpec((1, s, LANES), lambda p, bi: (bi, 0, 2 * np_ + p)),
            pl.BlockSpec((HEADS_PER_LANE_TILE, 1, BAND_TBL + BAND_QBLK), lambda p, bi: (p, 0, 0)),
        ],
        out_specs=pl.BlockSpec((1, s, LANES), lambda p, bi: (bi, 0, p)),
        out_shape=jax.ShapeDtypeStruct((b, s, D_MODEL), BF16),
        scratch_shapes=[
            pltpu.VMEM((HEADS_PER_LANE_TILE, BAND_QBLK, BAND_TBL), F32),
            pltpu.VMEM(((BAND_CHUNKS * CHUNK // BAND_QBLK + 1) * HEADS_PER_LANE_TILE,
                        BAND_QBLK, BAND_WIN), F32),
        ],
        compiler_params=_cparams("parallel", "arbitrary"),
        name="band_attention",
    )(qkv, qkv, qkv, g)


def kernel(x, c, ada_w, ada_b, norm_g, ffn_w_in, ffn_w_out, sb_w_qkv, sb_w_o, sg_w_in, sg_ln_g, sg_w_s, sg_bias, sg_w_out, sc_w_in, sc_conv_w, sc_w_out, cb_w_qkv, cb_rel_bias, cb_w_o):
    assert x.shape[1] % ROW_TILE == 0 and x.shape[2] == D_MODEL
    mod_all = _ada_modulation(c, ada_w, ada_b)
    for i in range(DEPTH):
        m, r = i % N_MIXERS, i // N_MIXERS
        mod = mod_all[i]
        g_mix = norm_g[i, 0:2]
        attn_out = w_o = None
        if m == 0:
            qkv = _norm_mod_proj(x, mod, norm_g[i, 0], sb_w_qkv[r], 0)
            attn_out, w_o = _stick_breaking_attention(qkv), sb_w_o[r]
        elif m == 1:
            x = _sgu_sublayer(x, mod, g_mix, sg_w_in[r], sg_ln_g[r], sg_w_s[r], sg_bias[r],
                              sg_w_out[r])
        elif m == 2:
            x = _conv_sublayer(x, mod, g_mix, sc_w_in[r], sc_conv_w[r], sc_w_out[r])
        else:
            qkv = _norm_mod_proj(x, mod, norm_g[i, 0], cb_w_qkv[r], 0)
            attn_out, w_o = _band_attention(qkv, cb_rel_bias[r]), cb_w_o[r]
        x = _ffn_sublayer(x, mod, norm_g[i], ffn_w_in, ffn_w_out, i, attn_out, w_o)
    return x
```

```python
import functools
import math

import jax
import jax.numpy as jnp
from jax import lax
from jax.experimental import pallas as pl
from jax.experimental.pallas import tpu as pltpu

D_MODEL = 1024
DEPTH = 4
N_MIXERS = 4
N_HEADS = 16
HEAD_DIM = D_MODEL // N_HEADS
CHUNK = 64
GMLP_BLOCK = 128
GMLP_WIDTH = 2 * D_MODEL
GMLP_GROUPS = 8
GMLP_GROUP_CH = GMLP_WIDTH // GMLP_GROUPS
CONV_WIDTH = 3
BAND_CHUNKS = 8
REL_CLIP = 128
D_FF = 2816
EPS = 1e-6

LANES = 128
HEADS_PER_LANE_TILE = LANES // HEAD_DIM
N_HEAD_PAIRS = N_HEADS // HEADS_PER_LANE_TILE
VMEM_LIMIT = 56 * 1024 * 1024
FFN_VMEM_LIMIT = 60 * 1024 * 1024

ROW_TILE = 512
FFN_CHUNK = 512
SB_QBLK = 128
SB_KBLK = 256
SB_EXP_ZERO = 105.0
SB_SLACK_PER_SCORE = 2.0 ** -6
SB_SLACK_CONST = 0.1
BAND_QBLK = 4 * CHUNK
BAND_GROUP = 14
BAND_WIN = BAND_CHUNKS * CHUNK + BAND_QBLK
BAND_TBL = BAND_WIN + BAND_CHUNKS * CHUNK
NEG_BIG = -1e30

BF16 = jnp.bfloat16
F32 = jnp.float32


def _cparams(*sem):
    return pltpu.CompilerParams(dimension_semantics=sem, vmem_limit_bytes=VMEM_LIMIT)


def _resident(shape):
    zeros = (0,) * len(shape)
    return pl.BlockSpec(shape, lambda *_: zeros, pipeline_mode=pl.Buffered(1))


def _resident_layer(shape, layer):
    index = (layer,) + (0,) * len(shape)
    return pl.BlockSpec((pl.Squeezed(),) + tuple(shape), lambda *_: index,
                        pipeline_mode=pl.Buffered(1))


def _rms_scale(v):
    return lax.rsqrt(jnp.mean(v * v, axis=-1, keepdims=True) + EPS)


def _prologue(x, g, shift, scale):
    y = x * _rms_scale(x) * g
    return (y * (1.0 + scale) + shift).astype(BF16)


def _epilogue(x, y, g, gate):
    return x + gate * (y * _rms_scale(y) * g)


def _ada_kernel(c_ref, w_ref, b_ref, o_ref):
    c = c_ref[...]
    a = (c * jax.nn.sigmoid(c)).astype(BF16)
    w = w_ref[0].astype(BF16)
    o_ref[0] = jnp.dot(a, w, preferred_element_type=F32) + b_ref[0]


def _ada_modulation(c, ada_w, ada_b):
    b = c.shape[0]
    rows = 8
    n = ada_w.shape[-1]
    tn = 1536
    c_pad = jnp.zeros((rows, D_MODEL), F32).at[:b].set(c)
    out = pl.pallas_call(
        _ada_kernel,
        grid=(DEPTH, n // tn),
        in_specs=[
            pl.BlockSpec((rows, D_MODEL), lambda l, j: (0, 0)),
            pl.BlockSpec((1, D_MODEL, tn), lambda l, j: (l, 0, j)),
            pl.BlockSpec((1, 1, tn), lambda l, j: (l, 0, j)),
        ],
        out_specs=pl.BlockSpec((1, rows, tn), lambda l, j: (l, 0, j)),
        out_shape=jax.ShapeDtypeStruct((DEPTH, rows, n), F32),
        compiler_params=_cparams("parallel", "parallel"),
        name="ada_modulation",
    )(c_pad, ada_w, ada_b.reshape(DEPTH, 1, n))
    return out[:, :b].reshape(DEPTH, b, 6, D_MODEL)


def _proj_kernel(x_ref, mod_ref, g_ref, w_ref, o_ref, *, mod_row, n_chunk):
    h = _prologue(x_ref[0], g_ref[...], mod_ref[0, mod_row:mod_row + 1, :],
                  mod_ref[0, mod_row + 1:mod_row + 2, :])
    n = w_ref.shape[1]
    for c in range(n // n_chunk):
        sl = slice(c * n_chunk, (c + 1) * n_chunk)
        o_ref[0, :, sl] = jnp.dot(h, w_ref[:, sl].astype(BF16),
                                  preferred_element_type=F32).astype(BF16)


def _norm_mod_proj(x, mod, g, w, mod_row):
    b, s, d = x.shape
    n = w.shape[1]
    return pl.pallas_call(
        functools.partial(_proj_kernel, mod_row=mod_row, n_chunk=512),
        grid=(b, s // ROW_TILE),
        in_specs=[
            pl.BlockSpec((1, ROW_TILE, d), lambda bi, i: (bi, i, 0)),
            pl.BlockSpec((1, 6, d), lambda bi, i: (bi, 0, 0)),
            _resident((1, d)),
            _resident((d, n)),
        ],
        out_specs=pl.BlockSpec((1, ROW_TILE, n), lambda bi, i: (bi, i, 0)),
        out_shape=jax.ShapeDtypeStruct((b, s, n), BF16),
        compiler_params=_cparams("parallel", "parallel"),
        name="norm_mod_proj",
    )(x, mod, g.reshape(1, d), w)


def _ffn_body(x, mod_ref, g_ref, win_ref, wout_ref, acc_ref):
    h = _prologue(x, g_ref[2:3, :], mod_ref[0, 3:4, :], mod_ref[0, 4:5, :])
    for c0 in range(0, D_FF, FFN_CHUNK):
        w = min(FFN_CHUNK, D_FF - c0)
        gate = jnp.dot(h, win_ref[:, c0:c0 + w].astype(BF16), preferred_element_type=F32)
        up = jnp.dot(h, win_ref[:, D_FF + c0:D_FF + c0 + w].astype(BF16),
                     preferred_element_type=F32)
        a = (gate * jax.nn.sigmoid(gate) * up).astype(BF16)
        part = jnp.dot(a, wout_ref[c0:c0 + w, :].astype(BF16), preferred_element_type=F32)
        if c0 == 0:
            acc_ref[...] = part
        else:
            acc_ref[...] += part
    return _epilogue(x, acc_ref[...], g_ref[3:4, :], mod_ref[0, 5:6, :])


def _ffn_kernel(x_ref, mod_ref, g_ref, win_ref, wout_ref, o_ref, acc_ref):
    o_ref[0] = _ffn_body(x_ref[0], mod_ref, g_ref, win_ref, wout_ref, acc_ref)


def _proj_ffn_kernel(a_ref, x_ref, mod_ref, g_ref, wo_ref, win_ref, wout_ref, o_ref,
                     acc_ref, mid_ref):
    y = jnp.dot(a_ref[0], wo_ref[...], preferred_element_type=F32)
    mid_ref[...] = _epilogue(x_ref[0], y, g_ref[1:2, :], mod_ref[0, 2:3, :])
    o_ref[0] = _ffn_body(mid_ref[...], mod_ref, g_ref, win_ref, wout_ref, acc_ref)


def _ffn_sublayer(x, mod, g4, w_in, w_out, layer, attn_out=None, w_o=None):
    b, s, d = x.shape
    row_spec = pl.BlockSpec((1, ROW_TILE, d), lambda bi, i: (bi, i, 0))
    common = [
        pl.BlockSpec((1, 6, d), lambda bi, i: (bi, 0, 0)),
        _resident((4, d)),
    ]
    weights = [_resident_layer((d, 2 * D_FF), layer), _resident_layer((D_FF, d), layer)]
    scratch = [pltpu.VMEM((ROW_TILE, d), F32)]
    if attn_out is None:
        body, in_specs, args = _ffn_kernel, [row_spec] + common + weights, (x, mod, g4, w_in, w_out)
    else:
        body = _proj_ffn_kernel
        in_specs = [row_spec, row_spec] + common + [_resident((d, d))] + weights
        args = (attn_out, x, mod, g4, w_o.astype(BF16), w_in, w_out)
        scratch = scratch + [pltpu.VMEM((ROW_TILE, d), F32)]
    return pl.pallas_call(
        body,
        grid=(b, s // ROW_TILE),
        in_specs=in_specs,
        out_specs=row_spec,
        out_shape=jax.ShapeDtypeStruct((b, s, d), F32),
        scratch_shapes=scratch,
        compiler_params=pltpu.CompilerParams(
            dimension_semantics=("parallel", "parallel"), vmem_limit_bytes=FFN_VMEM_LIMIT),
        name="ffn_sublayer" if attn_out is None else "proj_ffn_sublayer",
    )(*args)


def _gelu_tanh(x):
    c = 0.7978845608028654
    return 0.5 * x * (1.0 + jnp.tanh(c * (x + 0.044715 * (x * x * x))))


def _sgu_kernel(x_ref, mod_ref, g_ref, win_ref, lng_ref, ws_ref, sb_ref, wout_ref,
                o_ref, v_ref, y_ref):
    x = x_ref[0]
    h = _prologue(x, g_ref[0:1, :], mod_ref[0, 0:1, :], mod_ref[0, 1:2, :])
    n_chunk = 512
    half = GMLP_WIDTH // n_chunk
    for c in range(half):
        sl = slice(GMLP_WIDTH + c * n_chunk, GMLP_WIDTH + (c + 1) * n_chunk)
        v_ref[:, c * n_chunk:(c + 1) * n_chunk] = _gelu_tanh(
            jnp.dot(h, win_ref[:, sl].astype(BF16), preferred_element_type=F32))
    v = v_ref[...]
    mu = jnp.mean(v, axis=-1, keepdims=True)
    vc = v - mu
    var = jnp.mean(vc * vc, axis=-1, keepdims=True)
    vn = (vc * lax.rsqrt(var + EPS) * lng_ref[...]).astype(BF16)
    r = lax.broadcasted_iota(jnp.int32, (GMLP_BLOCK, GMLP_BLOCK), 0) // CHUNK
    cidx = lax.broadcasted_iota(jnp.int32, (GMLP_BLOCK, GMLP_BLOCK), 1) // CHUNK
    keep = r >= cidx
    for gi in range(GMLP_GROUPS):
        cs = slice(gi * GMLP_GROUP_CH, (gi + 1) * GMLP_GROUP_CH)
        u = _gelu_tanh(jnp.dot(h, win_ref[:, cs].astype(BF16), preferred_element_type=F32))
        ws = jnp.where(keep, ws_ref[gi], 0.0).astype(BF16)
        bias = sb_ref[:, gi:gi + 1]
        for blk in range(ROW_TILE // GMLP_BLOCK):
            rs = slice(blk * GMLP_BLOCK, (blk + 1) * GMLP_BLOCK)
            sv = jnp.dot(ws, vn[rs, cs], preferred_element_type=F32) + bias
            y_ref[rs, cs] = (u[rs] * sv).astype(BF16)
    y = jnp.dot(y_ref[...], wout_ref[...].astype(BF16), preferred_element_type=F32)
    o_ref[0] = _epilogue(x, y, g_ref[1:2, :], mod_ref[0, 2:3, :])


def _sgu_sublayer(x, mod, g2, w_in, ln_g, w_s, s_bias, w_out):
    b, s, d = x.shape
    return pl.pallas_call(
        _sgu_kernel,
        grid=(b, s // ROW_TILE),
        in_specs=[
            pl.BlockSpec((1, ROW_TILE, d), lambda bi, i: (bi, i, 0)),
            pl.BlockSpec((1, 6, d), lambda bi, i: (bi, 0, 0)),
            _resident((2, d)),
            _resident((d, 2 * GMLP_WIDTH)),
            _resident((1, GMLP_WIDTH)),
            _resident((GMLP_GROUPS, GMLP_BLOCK, GMLP_BLOCK)),
            _resident((GMLP_BLOCK, GMLP_GROUPS)),
            _resident((GMLP_WIDTH, d)),
        ],
        out_specs=pl.BlockSpec((1, ROW_TILE, d), lambda bi, i: (bi, i, 0)),
        out_shape=jax.ShapeDtypeStruct((b, s, d), F32),
        scratch_shapes=[pltpu.VMEM((ROW_TILE, GMLP_WIDTH), F32),
                        pltpu.VMEM((ROW_TILE, GMLP_WIDTH), BF16)],
        compiler_params=_cparams("parallel", "parallel"),
        name="sgu_sublayer",
    )(x, mod, g2, w_in, ln_g.reshape(1, GMLP_WIDTH), w_s, s_bias.T, w_out)


CONV_HALO = 8


def _conv_kernel(x_ref, mod_ref, g_ref, win_ref, cw_ref, wout_ref, o_ref, ext_ref, t_ref):
    i = pl.program_id(1)
    x = x_ref[0]
    h = _prologue(x, g_ref[0:1, :], mod_ref[0, 0:1, :], mod_ref[0, 1:2, :])

    @pl.when(i == 0)
    def _():
        ext_ref[0:CONV_HALO, :] = jnp.zeros((CONV_HALO, D_MODEL), F32)

    @pl.when(i != 0)
    def _():
        ext_ref[0:CONV_HALO, :] = ext_ref[ROW_TILE:ROW_TILE + CONV_HALO, :]

    n_chunk = 512
    for c in range(D_MODEL // n_chunk):
        cs = slice(c * n_chunk, (c + 1) * n_chunk)
        gate_c = jnp.dot(h, win_ref[:, D_MODEL + c * n_chunk:D_MODEL + (c + 1) * n_chunk]
                         .astype(BF16), preferred_element_type=F32)
        xt = jnp.dot(h, win_ref[:, 2 * D_MODEL + c * n_chunk:2 * D_MODEL + (c + 1) * n_chunk]
                     .astype(BF16), preferred_element_type=F32)
        ext_ref[CONV_HALO:, cs] = gate_c * xt
    for c in range(D_MODEL // n_chunk):
        cs = slice(c * n_chunk, (c + 1) * n_chunk)
        gate_b = jnp.dot(h, win_ref[:, cs].astype(BF16), preferred_element_type=F32)
        yc = (cw_ref[0:1, cs] * ext_ref[CONV_HALO - 2:CONV_HALO - 2 + ROW_TILE, cs]
              + cw_ref[1:2, cs] * ext_ref[CONV_HALO - 1:CONV_HALO - 1 + ROW_TILE, cs]
              + cw_ref[2:3, cs] * ext_ref[CONV_HALO:, cs])
        t_ref[:, cs] = (gate_b * yc).astype(BF16)
    y = jnp.dot(t_ref[...], wout_ref[...].astype(BF16), preferred_element_type=F32)
    o_ref[0] = _epilogue(x, y, g_ref[1:2, :], mod_ref[0, 2:3, :])


def _conv_sublayer(x, mod, g2, w_in, conv_w, w_out):
    b, s, d = x.shape
    return pl.pallas_call(
        _conv_kernel,
        grid=(b, s // ROW_TILE),
        in_specs=[
            pl.BlockSpec((1, ROW_TILE, d), lambda bi, i: (bi, i, 0)),
            pl.BlockSpec((1, 6, d), lambda bi, i: (bi, 0, 0)),
            _resident((2, d)),
            _resident((d, 3 * d)),
            _resident((CONV_WIDTH, d)),
            _resident((d, d)),
        ],
        out_specs=pl.BlockSpec((1, ROW_TILE, d), lambda bi, i: (bi, i, 0)),
        out_shape=jax.ShapeDtypeStruct((b, s, d), F32),
        scratch_shapes=[pltpu.VMEM((ROW_TILE + CONV_HALO, d), F32),
                        pltpu.VMEM((ROW_TILE, d), BF16)],
        compiler_params=_cparams("parallel", "arbitrary"),
        name="conv_sublayer",
    )(x, mod, g2, w_in, conv_w, w_out)


def _sb_kernel(q_ref, k_ref, v_ref, o_ref, qm_ref, acc_ref, run_ref, slack_ref):
    s = q_ref.shape[1]
    nh = HEADS_PER_LANE_TILE
    n_chunks = s // SB_QBLK
    rows_per_chunk = SB_QBLK * nh
    n_steps = (n_chunks - 1) // 2 + 1
    lane = lax.broadcasted_iota(jnp.int32, (1, LANES), 1)
    head_masks = [(lane // HEAD_DIM == hh) for hh in range(nh)]
    tri = jnp.where(lax.broadcasted_iota(jnp.int32, (SB_KBLK, SB_KBLK), 0)
                    >= lax.broadcasted_iota(jnp.int32, (SB_KBLK, SB_KBLK), 1),
                    1.0, 0.0).astype(BF16)
    row = lax.broadcasted_iota(jnp.int32, (SB_QBLK, SB_KBLK), 0)
    col = lax.broadcasted_iota(jnp.int32, (SB_QBLK, SB_KBLK), 1)
    key_in_first_half = lax.broadcasted_iota(jnp.int32, (SB_KBLK, 1), 0) < SB_QBLK
    col_in_first_half = col < SB_QBLK
    scale = HEAD_DIM ** -0.5
    big = jnp.full((8, LANES), jnp.inf, F32)

    kf = k_ref[0].astype(F32)
    k_norm_max = jnp.sqrt(jnp.max(jnp.sum(kf * kf, axis=-1, keepdims=True),
                                  axis=0, keepdims=True))

    def setup(c):
        q = q_ref[0, c * SB_QBLK:(c + 1) * SB_QBLK, :] * scale
        for hh in range(nh):
            r0 = c * rows_per_chunk + hh * SB_QBLK
            qh = jnp.where(head_masks[hh], q, jnp.zeros_like(q))
            qm_ref[r0:r0 + SB_QBLK, :] = qh
            qf = qh.astype(F32)
            z_bound = jnp.sqrt(jnp.sum(qf * qf, axis=-1, keepdims=True)) * k_norm_max
            slack_ref[r0:r0 + SB_QBLK, :] = jnp.broadcast_to(
                z_bound * SB_SLACK_PER_SCORE + SB_SLACK_CONST, (SB_QBLK, LANES))
        rs = slice(c * rows_per_chunk, (c + 1) * rows_per_chunk)
        acc_ref[rs, :] = jnp.zeros((rows_per_chunk, LANES), F32)
        run_ref[rs, :] = jnp.zeros((rows_per_chunk, LANES), F32)

    def step(d):
        rows = [slice(hh * SB_QBLK, (hh + 1) * SB_QBLK) for hh in range(nh)]
        per_chunk, sps = [], []
        for c in range(n_chunks):
            rs = slice(c * rows_per_chunk, (c + 1) * rows_per_chunk)
            if d is None:
                k0 = max(c * SB_QBLK - SB_QBLK, 0)
                mask = col < row + (c * SB_QBLK - k0)
                active = clipped = None
                more_keys = c >= 2
            else:
                nominal = c * SB_QBLK - SB_QBLK - d * SB_KBLK
                k0 = pl.multiple_of(jnp.maximum(nominal, 0), SB_QBLK)
                mask = None
                active = c >= 2 * d
                more_keys = c >= 2 * d + 2
                clipped = None if c % 2 else 2 * d == c
            kb = k_ref[0, pl.ds(k0, SB_KBLK), :]
            vb = v_ref[0, pl.ds(k0, SB_KBLK), :]
            if clipped is not None:
                key_ok = key_in_first_half | jnp.logical_not(clipped)
                kb = jnp.where(key_ok, kb, jnp.zeros_like(kb))
                vb = jnp.where(key_ok, vb, jnp.zeros_like(vb))
                mask = col_in_first_half | jnp.logical_not(clipped)
            z = lax.dot_general(qm_ref[rs, :], kb, (((1,), (1,)), ((), ())),
                                preferred_element_type=F32)
            for r in rows:
                zb = z[r].astype(BF16)
                sp = jnp.maximum(zb, 0.0) + jnp.log(1.0 + jnp.exp(-jnp.abs(zb)))
                if mask is not None:
                    sp = jnp.where(mask, sp, jnp.zeros_like(sp))
                sps.append(sp)
            per_chunk.append((rs, z, vb, mask if d is None else None, active, more_keys))
        rest_all = jnp.dot(jnp.concatenate(sps, axis=0), tri, preferred_element_type=F32)
        gap = big
        for c, (rs, z, vb, mask, active, more_keys) in enumerate(per_chunk):
            rest = rest_all[rs]
            run = run_ref[rs, :]
            a_s = []
            for r in rows:
                run_c = jnp.concatenate([run[r]] * (SB_KBLK // LANES), axis=1)
                a = jnp.exp(z[r] - rest[r] - run_c)
                if mask is not None:
                    a = jnp.where(mask, a, 0.0)
                a_s.append(a.astype(BF16))
            part = jnp.dot(jnp.concatenate(a_s, axis=0), vb, preferred_element_type=F32)
            total = jnp.broadcast_to(rest[:, 0:1], run.shape)
            if active is not None:
                part = jnp.where(active, part, 0.0)
                total = jnp.where(active, total, 0.0)
            acc_ref[rs, :] += part
            run = run + total
            run_ref[rs, :] = run
            if more_keys is False:
                continue
            chunk_gap = jnp.min((run - slack_ref[rs, :]).reshape(-1, 8, LANES), axis=0)
            if more_keys is not True:
                chunk_gap = jnp.where(more_keys, chunk_gap, big)
            gap = jnp.minimum(gap, chunk_gap)
        return gap

    def margin(gap):
        return jnp.min(gap, axis=0, keepdims=True)[0, 0]

    for c in range(n_chunks):
        setup(c)
    gap0 = step(None)

    def more_keys_matter(carry):
        d, mg = carry
        return (d < n_steps) & (mg < SB_EXP_ZERO)

    def walk(carry):
        d, _ = carry
        return d + 1, margin(step(d))

    lax.while_loop(more_keys_matter, walk, (jnp.int32(1), margin(gap0)))

    for c in range(n_chunks):
        r0 = c * rows_per_chunk
        out = acc_ref[r0:r0 + SB_QBLK, :]
        for hh in range(1, nh):
            out = jnp.where(head_masks[hh],
                            acc_ref[r0 + hh * SB_QBLK:r0 + (hh + 1) * SB_QBLK, :], out)
        o_ref[0, c * SB_QBLK:(c + 1) * SB_QBLK, :] = out.astype(BF16)


def _stick_breaking_attention(qkv):
    b, s, _ = qkv.shape
    np_ = N_HEAD_PAIRS
    assert s % SB_KBLK == 0
    rows = s * HEADS_PER_LANE_TILE
    return pl.pallas_call(
        _sb_kernel,
        grid=(b, np_),
        in_specs=[
            pl.BlockSpec((1, s, LANES), lambda bi, p: (bi, 0, p)),
            pl.BlockSpec((1, s, LANES), lambda bi, p: (bi, 0, np_ + p)),
            pl.BlockSpec((1, s, LANES), lambda bi, p: (bi, 0, 2 * np_ + p)),
        ],
        out_specs=pl.BlockSpec((1, s, LANES), lambda bi, p: (bi, 0, p)),
        out_shape=jax.ShapeDtypeStruct((b, s, D_MODEL), BF16),
        scratch_shapes=[
            pltpu.VMEM((rows, LANES), BF16),
            pltpu.VMEM((rows, LANES), F32),
            pltpu.VMEM((rows, LANES), F32),
            pltpu.VMEM((rows, LANES), F32),
        ],
        compiler_params=_cparams("parallel", "parallel"),
        name="stick_breaking_attention",
    )(qkv, qkv, qkv)


def _band_kernel(q_ref, k_ref, v_ref, g_ref, o_ref, tbl_ref, bias_ref):
    s = q_ref.shape[1]
    lead = BAND_CHUNKS * CHUNK
    nh = HEADS_PER_LANE_TILE
    n_head = lead // BAND_QBLK
    chunks_per_blk = BAND_QBLK // CHUNK
    lane = lax.broadcasted_iota(jnp.int32, (1, LANES), 1)
    head_masks = [(lane // HEAD_DIM == hh) for hh in range(nh)]
    scale = HEAD_DIM ** -0.5

    @pl.when(pl.program_id(1) == 0)
    def _():
        for hh in range(nh):
            g_rows = jnp.broadcast_to(g_ref[hh], (BAND_QBLK, BAND_TBL + BAND_QBLK))
            tbl_ref[hh] = pltpu.roll(g_rows, 0, 1, stride=1, stride_axis=0)[:, BAND_QBLK:]
        row_chunk = lax.broadcasted_iota(jnp.int32, (BAND_QBLK, BAND_WIN), 0) // CHUNK
        col_chunk = lax.broadcasted_iota(jnp.int32, (BAND_QBLK, BAND_WIN), 1) // CHUNK
        for slot in range(n_head + 1):
            first_key_chunk_minus_query_chunk = -min(slot * chunks_per_blk, BAND_CHUNKS)
            tbl_off = lead + first_key_chunk_minus_query_chunk * CHUNK
            rel = col_chunk + first_key_chunk_minus_query_chunk - row_chunk
            valid = (rel <= 0) & (rel >= -BAND_CHUNKS)
            for hh in range(nh):
                bias_ref[slot * nh + hh] = jnp.where(
                    valid, tbl_ref[hh, :, tbl_off:tbl_off + BAND_WIN], NEG_BIG)

    def q_block(q0, k0, slot):
        q = q_ref[0, pl.ds(q0, BAND_QBLK), :] * scale
        zero = jnp.zeros_like(q)
        kb = k_ref[0, pl.ds(k0, BAND_WIN), :]
        vb = v_ref[0, pl.ds(k0, BAND_WIN), :]
        qh = jnp.concatenate([jnp.where(head_masks[hh], q, zero) for hh in range(nh)], axis=0)
        sc = lax.dot_general(qh, kb, (((1,), (1,)), ((), ())), preferred_element_type=F32)
        sc = sc + jnp.concatenate([bias_ref[slot * nh + hh] for hh in range(nh)], axis=0)
        m = jnp.max(sc, axis=-1, keepdims=True)
        e = jnp.exp(sc - m)
        denom = jnp.sum(e, axis=-1, keepdims=True)
        oh = jnp.dot(e.astype(BF16), vb, preferred_element_type=F32) / denom
        out = oh[0:BAND_QBLK]
        for hh in range(1, nh):
            out = jnp.where(head_masks[hh], oh[hh * BAND_QBLK:(hh + 1) * BAND_QBLK], out)
        o_ref[0, pl.ds(q0, BAND_QBLK), :] = out.astype(BF16)

    for i in range(n_head):
        q_block(i * BAND_QBLK, 0, i)

    n_general = s // BAND_QBLK - n_head
    group = math.gcd(n_general, BAND_GROUP)

    def body(t, _):
        for u in range(group):
            q0 = pl.multiple_of((n_head + t * group + u) * BAND_QBLK, BAND_QBLK)
            q_block(q0, pl.multiple_of(q0 - lead, BAND_QBLK), n_head)
        return 0

    lax.fori_loop(0, n_general // group, body, 0)


def _band_bias_vector(rel_bias):
    lead = BAND_CHUNKS * CHUNK
    span = BAND_TBL + BAND_QBLK
    n_tab = 2 * REL_CLIP + 1
    n_hi = BAND_QBLK + lead - REL_CLIP
    g = jnp.concatenate([jnp.repeat(rel_bias[:, -1:], n_hi, axis=1), rel_bias[:, ::-1],
                         jnp.repeat(rel_bias[:, :1], span - n_hi - n_tab, axis=1)], axis=1)
    return g.reshape(-1, 1, span).astype(F32)


def _band_attention(qkv, rel_bias):
    b, s, _ = qkv.shape
    np_ = N_HEAD_PAIRS
    g = _band_bias_vector(rel_bias)
    return pl.pallas_call(
        _band_kernel,
        grid=(np_, b),
        in_specs=[
            pl.BlockSpec((1, s, LANES), lambda p, bi: (bi, 0, p)),
            pl.BlockSpec((1, s, LANES), lambda p, bi: (bi, 0, np_ + p)),
            pl.BlockSpec((1, s, LANES), lambda p, bi: (bi, 0, 2 * np_ + p)),
            pl.BlockSpec((HEADS_PER_LANE_TILE, 1, BAND_TBL + BAND_QBLK), lambda p, bi: (p, 0, 0)),
        ],
        out_specs=pl.BlockSpec((1, s, LANES), lambda p, bi: (bi, 0, p)),
        out_shape=jax.ShapeDtypeStruct((b, s, D_MODEL), BF16),
        scratch_shapes=[
            pltpu.VMEM((HEADS_PER_LANE_TILE, BAND_QBLK, BAND_TBL), F32),
            pltpu.VMEM(((BAND_CHUNKS * CHUNK // BAND_QBLK + 1) * HEADS_PER_LANE_TILE,
                        BAND_QBLK, BAND_WIN), F32),
        ],
        compiler_params=_cparams("parallel", "arbitrary"),
        name="band_attention",
    )(qkv, qkv, qkv, g)


def kernel(x, c, ada_w, ada_b, norm_g, ffn_w_in, ffn_w_out, sb_w_qkv, sb_w_o, sg_w_in, sg_ln_g, sg_w_s, sg_bias, sg_w_out, sc_w_in, sc_conv_w, sc_w_out, cb_w_qkv, cb_rel_bias, cb_w_o):
    assert x.shape[1] % ROW_TILE == 0 and x.shape[2] == D_MODEL
    mod_all = _ada_modulation(c, ada_w, ada_b)
    for i in range(DEPTH):
        m, r = i % N_MIXERS, i // N_MIXERS
        mod = mod_all[i]
        g_mix = norm_g[i, 0:2]
        attn_out = w_o = None
        if m == 0:
            qkv = _norm_mod_proj(x, mod, norm_g[i, 0], sb_w_qkv[r], 0)
            attn_out, w_o = _stick_breaking_attention(qkv), sb_w_o[r]
        elif m == 1:
            x = _sgu_sublayer(x, mod, g_mix, sg_w_in[r], sg_ln_g[r], sg_w_s[r], sg_bias[r],
                              sg_w_out[r])
        elif m == 2:
            x = _conv_sublayer(x, mod, g_mix, sc_w_in[r], sc_conv_w[r], sc_w_out[r])
        else:
            qkv = _norm_mod_proj(x, mod, norm_g[i, 0], cb_w_qkv[r], 0)
            attn_out, w_o = _band_attention(qkv, cb_rel_bias[r]), cb_w_o[r]
        x = _ffn_sublayer(x, mod, norm_g[i], ffn_w_in, ffn_w_out, i, attn_out, w_o)
    return x
```

```python
import functools
import math

import jax
import jax.numpy as jnp
from jax import lax
from jax.experimental import pallas as pl
from jax.experimental.pallas import tpu as pltpu

D_MODEL = 1024
DEPTH = 4
N_MIXERS = 4
N_HEADS = 16
HEAD_DIM = D_MODEL // N_HEADS
CHUNK = 64
GMLP_BLOCK = 128
GMLP_WIDTH = 2 * D_MODEL
GMLP_GROUPS = 8
GMLP_GROUP_CH = GMLP_WIDTH // GMLP_GROUPS
CONV_WIDTH = 3
BAND_CHUNKS = 8
REL_CLIP = 128
D_FF = 2816
EPS = 1e-6

LANES = 128
HEADS_PER_LANE_TILE = LANES // HEAD_DIM
N_HEAD_PAIRS = N_HEADS // HEADS_PER_LANE_TILE
V7X_VMEM_BYTES = 64 * 1024 * 1024
VMEM_LIMIT = V7X_VMEM_BYTES - 8 * 1024 * 1024
FFN_VMEM_LIMIT = V7X_VMEM_BYTES - 4 * 1024 * 1024

ROW_TILE = 512
FFN_CHUNK = 512
SB_QBLK = 128
SB_KBLK = 256
SB_EXP_ZERO = 105.0
SB_SLACK_PER_SCORE = 2.0 ** -6
SB_SLACK_CONST = 0.1
BAND_QBLK = 4 * CHUNK
BAND_GROUP = 14
BAND_WIN = BAND_CHUNKS * CHUNK + BAND_QBLK
BAND_TBL = BAND_WIN + BAND_CHUNKS * CHUNK
NEG_BIG = -1e30

BF16 = jnp.bfloat16
F32 = jnp.float32


def _cparams(*sem):
    return pltpu.CompilerParams(dimension_semantics=sem, vmem_limit_bytes=VMEM_LIMIT)


def _resident(shape):
    zeros = (0,) * len(shape)
    return pl.BlockSpec(shape, lambda *_: zeros, pipeline_mode=pl.Buffered(1))


def _resident_layer(shape, layer):
    index = (layer,) + (0,) * len(shape)
    return pl.BlockSpec((pl.Squeezed(),) + tuple(shape), lambda *_: index,
                        pipeline_mode=pl.Buffered(1))


def _rms_scale(v):
    return lax.rsqrt(jnp.mean(v * v, axis=-1, keepdims=True) + EPS)


def _prologue(x, g, shift, scale):
    y = x * _rms_scale(x) * g
    return (y * (1.0 + scale) + shift).astype(BF16)


def _epilogue(x, y, g, gate):
    return x + gate * (y * _rms_scale(y) * g)


def _ada_kernel(c_ref, w_ref, b_ref, o_ref):
    c = c_ref[...]
    a = (c * jax.nn.sigmoid(c)).astype(BF16)
    w = w_ref[0].astype(BF16)
    o_ref[0] = jnp.dot(a, w, preferred_element_type=F32) + b_ref[0]


def _ada_modulation(c, ada_w, ada_b):
    b = c.shape[0]
    rows = 8
    n = ada_w.shape[-1]
    tn = 1536
    c_pad = jnp.zeros((rows, D_MODEL), F32).at[:b].set(c)
    out = pl.pallas_call(
        _ada_kernel,
        grid=(DEPTH, n // tn),
        in_specs=[
            pl.BlockSpec((rows, D_MODEL), lambda l, j: (0, 0)),
            pl.BlockSpec((1, D_MODEL, tn), lambda l, j: (l, 0, j)),
            pl.BlockSpec((1, 1, tn), lambda l, j: (l, 0, j)),
        ],
        out_specs=pl.BlockSpec((1, rows, tn), lambda l, j: (l, 0, j)),
        out_shape=jax.ShapeDtypeStruct((DEPTH, rows, n), F32),
        compiler_params=_cparams("parallel", "parallel"),
        name="ada_modulation",
    )(c_pad, ada_w, ada_b.reshape(DEPTH, 1, n))
    return out[:, :b].reshape(DEPTH, b, 6, D_MODEL)


def _proj_kernel(x_ref, mod_ref, g_ref, w_ref, o_ref, *, mod_row, n_chunk):
    h = _prologue(x_ref[0], g_ref[...], mod_ref[0, mod_row:mod_row + 1, :],
                  mod_ref[0, mod_row + 1:mod_row + 2, :])
    n = w_ref.shape[1]
    for c in range(n // n_chunk):
        sl = slice(c * n_chunk, (c + 1) * n_chunk)
        o_ref[0, :, sl] = jnp.dot(h, w_ref[:, sl].astype(BF16),
                                  preferred_element_type=F32).astype(BF16)


def _norm_mod_proj(x, mod, g, w, mod_row):
    b, s, d = x.shape
    n = w.shape[1]
    return pl.pallas_call(
        functools.partial(_proj_kernel, mod_row=mod_row, n_chunk=512),
        grid=(b, s // ROW_TILE),
        in_specs=[
            pl.BlockSpec((1, ROW_TILE, d), lambda bi, i: (bi, i, 0)),
            pl.BlockSpec((1, 6, d), lambda bi, i: (bi, 0, 0)),
            _resident((1, d)),
            _resident((d, n)),
        ],
        out_specs=pl.BlockSpec((1, ROW_TILE, n), lambda bi, i: (bi, i, 0)),
        out_shape=jax.ShapeDtypeStruct((b, s, n), BF16),
        compiler_params=_cparams("parallel", "parallel"),
        name="norm_mod_proj",
    )(x, mod, g.reshape(1, d), w)


def _ffn_body(x, mod_ref, g_ref, win_ref, wout_ref, acc_ref):
    h = _prologue(x, g_ref[2:3, :], mod_ref[0, 3:4, :], mod_ref[0, 4:5, :])
    for c0 in range(0, D_FF, FFN_CHUNK):
        w = min(FFN_CHUNK, D_FF - c0)
        gate = jnp.dot(h, win_ref[:, c0:c0 + w].astype(BF16), preferred_element_type=F32)
        up = jnp.dot(h, win_ref[:, D_FF + c0:D_FF + c0 + w].astype(BF16),
                     preferred_element_type=F32)
        a = (gate * jax.nn.sigmoid(gate) * up).astype(BF16)
        part = jnp.dot(a, wout_ref[c0:c0 + w, :].astype(BF16), preferred_element_type=F32)
        if c0 == 0:
            acc_ref[...] = part
        else:
            acc_ref[...] += part
    return _epilogue(x, acc_ref[...], g_ref[3:4, :], mod_ref[0, 5:6, :])


def _ffn_kernel(x_ref, mod_ref, g_ref, win_ref, wout_ref, o_ref, acc_ref):
    o_ref[0] = _ffn_body(x_ref[0], mod_ref, g_ref, win_ref, wout_ref, acc_ref)


def _proj_ffn_kernel(a_ref, x_ref, mod_ref, g_ref, wo_ref, win_ref, wout_ref, o_ref,
                     acc_ref, mid_ref):
    y = jnp.dot(a_ref[0], wo_ref[...], preferred_element_type=F32)
    mid_ref[...] = _epilogue(x_ref[0], y, g_ref[1:2, :], mod_ref[0, 2:3, :])
    o_ref[0] = _ffn_body(mid_ref[...], mod_ref, g_ref, win_ref, wout_ref, acc_ref)


def _ffn_sublayer(x, mod, g4, w_in, w_out, layer, attn_out=None, w_o=None):
    b, s, d = x.shape
    row_spec = pl.BlockSpec((1, ROW_TILE, d), lambda bi, i: (bi, i, 0))
    common = [
        pl.BlockSpec((1, 6, d), lambda bi, i: (bi, 0, 0)),
        _resident((4, d)),
    ]
    weights = [_resident_layer((d, 2 * D_FF), layer), _resident_layer((D_FF, d), layer)]
    scratch = [pltpu.VMEM((ROW_TILE, d), F32)]
    if attn_out is None:
        body, in_specs, args = _ffn_kernel, [row_spec] + common + weights, (x, mod, g4, w_in, w_out)
    else:
        body = _proj_ffn_kernel
        in_specs = [row_spec, row_spec] + common + [_resident((d, d))] + weights
        args = (attn_out, x, mod, g4, w_o.astype(BF16), w_in, w_out)
        scratch = scratch + [pltpu.VMEM((ROW_TILE, d), F32)]
    return pl.pallas_call(
        body,
        grid=(b, s // ROW_TILE),
        in_specs=in_specs,
        out_specs=row_spec,
        out_shape=jax.ShapeDtypeStruct((b, s, d), F32),
        scratch_shapes=scratch,
        compiler_params=pltpu.CompilerParams(
            dimension_semantics=("parallel", "parallel"), vmem_limit_bytes=FFN_VMEM_LIMIT),
        name="ffn_sublayer" if attn_out is None else "proj_ffn_sublayer",
    )(*args)


def _gelu_tanh(x):
    c = 0.7978845608028654
    return 0.5 * x * (1.0 + jnp.tanh(c * (x + 0.044715 * (x * x * x))))


def _sgu_kernel(x_ref, mod_ref, g_ref, win_ref, lng_ref, ws_ref, sb_ref, wout_ref,
                o_ref, u_ref, v_ref, y_ref):
    x = x_ref[0]
    h = _prologue(x, g_ref[0:1, :], mod_ref[0, 0:1, :], mod_ref[0, 1:2, :])
    n_chunk = 512
    half = GMLP_WIDTH // n_chunk
    for c in range(half):
        sl = slice(GMLP_WIDTH + c * n_chunk, GMLP_WIDTH + (c + 1) * n_chunk)
        v_ref[:, c * n_chunk:(c + 1) * n_chunk] = _gelu_tanh(
            jnp.dot(h, win_ref[:, sl].astype(BF16), preferred_element_type=F32))
    for c in range(half):
        sl = slice(c * n_chunk, (c + 1) * n_chunk)
        u_ref[:, sl] = _gelu_tanh(
            jnp.dot(h, win_ref[:, sl].astype(BF16), preferred_element_type=F32))
    v = v_ref[...]
    mu = jnp.mean(v, axis=-1, keepdims=True)
    vc = v - mu
    var = jnp.mean(vc * vc, axis=-1, keepdims=True)
    vn = (vc * lax.rsqrt(var + EPS) * lng_ref[...]).astype(BF16)
    r = lax.broadcasted_iota(jnp.int32, (GMLP_BLOCK, GMLP_BLOCK), 0) // CHUNK
    cidx = lax.broadcasted_iota(jnp.int32, (GMLP_BLOCK, GMLP_BLOCK), 1) // CHUNK
    keep = r >= cidx
    y = None
    for gi in range(GMLP_GROUPS):
        cs = slice(gi * GMLP_GROUP_CH, (gi + 1) * GMLP_GROUP_CH)
        ws = jnp.where(keep, ws_ref[gi], 0.0).astype(BF16)
        bias = sb_ref[:, gi:gi + 1]
        for blk in range(ROW_TILE // GMLP_BLOCK):
            rs = slice(blk * GMLP_BLOCK, (blk + 1) * GMLP_BLOCK)
            sv = jnp.dot(ws, vn[rs, cs], preferred_element_type=F32) + bias
            y_ref[rs, cs] = (u_ref[rs, cs] * sv).astype(BF16)
        if gi % 2:
            ks = slice((gi - 1) * GMLP_GROUP_CH, (gi + 1) * GMLP_GROUP_CH)
            part = jnp.dot(y_ref[:, ks], wout_ref[ks, :].astype(BF16),
                           preferred_element_type=F32)
            y = part if y is None else y + part
    o_ref[0] = _epilogue(x, y, g_ref[1:2, :], mod_ref[0, 2:3, :])


def _sgu_sublayer(x, mod, g2, w_in, ln_g, w_s, s_bias, w_out):
    b, s, d = x.shape
    return pl.pallas_call(
        _sgu_kernel,
        grid=(b, s // ROW_TILE),
        in_specs=[
            pl.BlockSpec((1, ROW_TILE, d), lambda bi, i: (bi, i, 0)),
            pl.BlockSpec((1, 6, d), lambda bi, i: (bi, 0, 0)),
            _resident((2, d)),
            _resident((d, 2 * GMLP_WIDTH)),
            _resident((1, GMLP_WIDTH)),
            _resident((GMLP_GROUPS, GMLP_BLOCK, GMLP_BLOCK)),
            _resident((GMLP_BLOCK, GMLP_GROUPS)),
            _resident((GMLP_WIDTH, d)),
        ],
        out_specs=pl.BlockSpec((1, ROW_TILE, d), lambda bi, i: (bi, i, 0)),
        out_shape=jax.ShapeDtypeStruct((b, s, d), F32),
        scratch_shapes=[pltpu.VMEM((ROW_TILE, GMLP_WIDTH), F32),
                        pltpu.VMEM((ROW_TILE, GMLP_WIDTH), F32),
                        pltpu.VMEM((ROW_TILE, GMLP_WIDTH), BF16)],
        compiler_params=_cparams("parallel", "parallel"),
        name="sgu_sublayer",
    )(x, mod, g2, w_in, ln_g.reshape(1, GMLP_WIDTH), w_s, s_bias.T, w_out)


CONV_HALO = 8


def _conv_kernel(x_ref, mod_ref, g_ref, win_ref, cw_ref, wout_ref, o_ref, ext_ref, t_ref):
    i = pl.program_id(1)
    x = x_ref[0]
    h = _prologue(x, g_ref[0:1, :], mod_ref[0, 0:1, :], mod_ref[0, 1:2, :])

    @pl.when(i == 0)
    def _():
        ext_ref[0:CONV_HALO, :] = jnp.zeros((CONV_HALO, D_MODEL), F32)

    @pl.when(i != 0)
    def _():
        ext_ref[0:CONV_HALO, :] = ext_ref[ROW_TILE:ROW_TILE + CONV_HALO, :]

    n_chunk = 512
    for c in range(D_MODEL // n_chunk):
        cs = slice(c * n_chunk, (c + 1) * n_chunk)
        gate_c = jnp.dot(h, win_ref[:, D_MODEL + c * n_chunk:D_MODEL + (c + 1) * n_chunk]
                         .astype(BF16), preferred_element_type=F32)
        xt = jnp.dot(h, win_ref[:, 2 * D_MODEL + c * n_chunk:2 * D_MODEL + (c + 1) * n_chunk]
                     .astype(BF16), preferred_element_type=F32)
        ext_ref[CONV_HALO:, cs] = gate_c * xt
    for c in range(D_MODEL // n_chunk):
        cs = slice(c * n_chunk, (c + 1) * n_chunk)
        gate_b = jnp.dot(h, win_ref[:, cs].astype(BF16), preferred_element_type=F32)
        yc = (cw_ref[0:1, cs] * ext_ref[CONV_HALO - 2:CONV_HALO - 2 + ROW_TILE, cs]
              + cw_ref[1:2, cs] * ext_ref[CONV_HALO - 1:CONV_HALO - 1 + ROW_TILE, cs]
              + cw_ref[2:3, cs] * ext_ref[CONV_HALO:, cs])
        t_ref[:, cs] = (gate_b * yc).astype(BF16)
    y = jnp.dot(t_ref[...], wout_ref[...].astype(BF16), preferred_element_type=F32)
    o_ref[0] = _epilogue(x, y, g_ref[1:2, :], mod_ref[0, 2:3, :])


def _conv_sublayer(x, mod, g2, w_in, conv_w, w_out):
    b, s, d = x.shape
    return pl.pallas_call(
        _conv_kernel,
        grid=(b, s // ROW_TILE),
        in_specs=[
            pl.BlockSpec((1, ROW_TILE, d), lambda bi, i: (bi, i, 0)),
            pl.BlockSpec((1, 6, d), lambda bi, i: (bi, 0, 0)),
            _resident((2, d)),
            _resident((d, 3 * d)),
            _resident((CONV_WIDTH, d)),
            _resident((d, d)),
        ],
        out_specs=pl.BlockSpec((1, ROW_TILE, d), lambda bi, i: (bi, i, 0)),
        out_shape=jax.ShapeDtypeStruct((b, s, d), F32),
        scratch_shapes=[pltpu.VMEM((ROW_TILE + CONV_HALO, d), F32),
                        pltpu.VMEM((ROW_TILE, d), BF16)],
        compiler_params=_cparams("parallel", "arbitrary"),
        name="conv_sublayer",
    )(x, mod, g2, w_in, conv_w, w_out)


def _sb_kernel(q_ref, k_ref, v_ref, o_ref, qm_ref, acc_ref, run_ref, slack_ref):
    s = q_ref.shape[1]
    nh = HEADS_PER_LANE_TILE
    n_chunks = s // SB_QBLK
    rows_per_chunk = SB_QBLK * nh
    n_steps = (n_chunks - 1) // 2 + 1
    lane = lax.broadcasted_iota(jnp.int32, (1, LANES), 1)
    head_masks = [(lane // HEAD_DIM == hh) for hh in range(nh)]
    tri = jnp.where(lax.broadcasted_iota(jnp.int32, (SB_KBLK, SB_KBLK), 0)
                    >= lax.broadcasted_iota(jnp.int32, (SB_KBLK, SB_KBLK), 1),
                    1.0, 0.0).astype(BF16)
    row = lax.broadcasted_iota(jnp.int32, (SB_QBLK, SB_KBLK), 0)
    col = lax.broadcasted_iota(jnp.int32, (SB_QBLK, SB_KBLK), 1)
    key_in_first_half = lax.broadcasted_iota(jnp.int32, (SB_KBLK, 1), 0) < SB_QBLK
    col_in_first_half = col < SB_QBLK
    scale = HEAD_DIM ** -0.5
    big = jnp.full((8, LANES), jnp.inf, F32)

    kf = k_ref[0].astype(F32)
    k_norm_max = jnp.sqrt(jnp.max(jnp.sum(kf * kf, axis=-1, keepdims=True),
                                  axis=0, keepdims=True))

    def setup(c):
        q = q_ref[0, c * SB_QBLK:(c + 1) * SB_QBLK, :] * scale
        for hh in range(nh):
            r0 = c * rows_per_chunk + hh * SB_QBLK
            qh = jnp.where(head_masks[hh], q, jnp.zeros_like(q))
            qm_ref[r0:r0 + SB_QBLK, :] = qh
            qf = qh.astype(F32)
            z_bound = jnp.sqrt(jnp.sum(qf * qf, axis=-1, keepdims=True)) * k_norm_max
            slack_ref[r0:r0 + SB_QBLK, :] = jnp.broadcast_to(
                z_bound * SB_SLACK_PER_SCORE + SB_SLACK_CONST, (SB_QBLK, LANES))
        rs = slice(c * rows_per_chunk, (c + 1) * rows_per_chunk)
        acc_ref[rs, :] = jnp.zeros((rows_per_chunk, LANES), F32)
        run_ref[rs, :] = jnp.zeros((rows_per_chunk, LANES), F32)

    def step(d):
        rows = [slice(hh * SB_QBLK, (hh + 1) * SB_QBLK) for hh in range(nh)]
        per_chunk, sps = [], []
        for c in range(n_chunks):
            rs = slice(c * rows_per_chunk, (c + 1) * rows_per_chunk)
            if d is None:
                k0 = max(c * SB_QBLK - SB_QBLK, 0)
                mask = col < row + (c * SB_QBLK - k0)
                active = clipped = None
                more_keys = c >= 2
            else:
                nominal = c * SB_QBLK - SB_QBLK - d * SB_KBLK
                k0 = pl.multiple_of(jnp.maximum(nominal, 0), SB_QBLK)
                mask = None
                active = c >= 2 * d
                more_keys = c >= 2 * d + 2
                clipped = None if c % 2 else 2 * d == c
            kb = k_ref[0, pl.ds(k0, SB_KBLK), :]
            vb = v_ref[0, pl.ds(k0, SB_KBLK), :]
            if clipped is not None:
                key_ok = key_in_first_half | jnp.logical_not(clipped)
                kb = jnp.where(key_ok, kb, jnp.zeros_like(kb))
                vb = jnp.where(key_ok, vb, jnp.zeros_like(vb))
                mask = col_in_first_half | jnp.logical_not(clipped)
            z = lax.dot_general(qm_ref[rs, :], kb, (((1,), (1,)), ((), ())),
                                preferred_element_type=F32)
            for r in rows:
                zb = z[r].astype(BF16)
                sp = jnp.maximum(zb, 0.0) + jnp.log(1.0 + jnp.exp(-jnp.abs(zb)))
                if mask is not None:
                    sp = jnp.where(mask, sp, jnp.zeros_like(sp))
                sps.append(sp)
            per_chunk.append((rs, z, vb, mask if d is None else None, active, more_keys))
        rest_all = jnp.dot(jnp.concatenate(sps, axis=0), tri, preferred_element_type=F32)
        gap = big
        for c, (rs, z, vb, mask, active, more_keys) in enumerate(per_chunk):
            rest = rest_all[rs]
            run = run_ref[rs, :]
            a_s = []
            for r in rows:
                run_c = jnp.concatenate([run[r]] * (SB_KBLK // LANES), axis=1)
                a = jnp.exp(z[r] - rest[r] - run_c)
                if mask is not None:
                    a = jnp.where(mask, a, 0.0)
                a_s.append(a.astype(BF16))
            part = jnp.dot(jnp.concatenate(a_s, axis=0), vb, preferred_element_type=F32)
            total = jnp.broadcast_to(rest[:, 0:1], run.shape)
            if active is not None:
                part = jnp.where(active, part, 0.0)
                total = jnp.where(active, total, 0.0)
            acc_ref[rs, :] += part
            run = run + total
            run_ref[rs, :] = run
            if more_keys is False:
                continue
            chunk_gap = jnp.min((run - slack_ref[rs, :]).reshape(-1, 8, LANES), axis=0)
            if more_keys is not True:
                chunk_gap = jnp.where(more_keys, chunk_gap, big)
            gap = jnp.minimum(gap, chunk_gap)
        return gap

    def margin(gap):
        return jnp.min(gap, axis=0, keepdims=True)[0, 0]

    for c in range(n_chunks):
        setup(c)
    gap0 = step(None)

    def more_keys_matter(carry):
        d, mg = carry
        return (d < n_steps) & (mg < SB_EXP_ZERO)

    def walk(carry):
        d, _ = carry
        return d + 1, margin(step(d))

    lax.while_loop(more_keys_matter, walk, (jnp.int32(1), margin(gap0)))

    for c in range(n_chunks):
        r0 = c * rows_per_chunk
        out = acc_ref[r0:r0 + SB_QBLK, :]
        for hh in range(1, nh):
            out = jnp.where(head_masks[hh],
                            acc_ref[r0 + hh * SB_QBLK:r0 + (hh + 1) * SB_QBLK, :], out)
        o_ref[0, c * SB_QBLK:(c + 1) * SB_QBLK, :] = out.astype(BF16)


def _stick_breaking_attention(qkv):
    b, s, _ = qkv.shape
    np_ = N_HEAD_PAIRS
    assert s % SB_KBLK == 0
    rows = s * HEADS_PER_LANE_TILE
    return pl.pallas_call(
        _sb_kernel,
        grid=(b, np_),
        in_specs=[
            pl.BlockSpec((1, s, LANES), lambda bi, p: (bi, 0, p)),
            pl.BlockSpec((1, s, LANES), lambda bi, p: (bi, 0, np_ + p)),
            pl.BlockSpec((1, s, LANES), lambda bi, p: (bi, 0, 2 * np_ + p)),
        ],
        out_specs=pl.BlockSpec((1, s, LANES), lambda bi, p: (bi, 0, p)),
        out_shape=jax.ShapeDtypeStruct((b, s, D_MODEL), BF16),
        scratch_shapes=[
            pltpu.VMEM((rows, LANES), BF16),
            pltpu.VMEM((rows, LANES), F32),
            pltpu.VMEM((rows, LANES), F32),
            pltpu.VMEM((rows, LANES), F32),
        ],
        compiler_params=_cparams("parallel", "parallel"),
        name="stick_breaking_attention",
    )(qkv, qkv, qkv)


def _band_kernel(q_ref, k_ref, v_ref, g_ref, o_ref, tbl_ref, bias_ref):
    s = q_ref.shape[1]
    lead = BAND_CHUNKS * CHUNK
    nh = HEADS_PER_LANE_TILE
    n_head = lead // BAND_QBLK
    chunks_per_blk = BAND_QBLK // CHUNK
    lane = lax.broadcasted_iota(jnp.int32, (1, LANES), 1)
    head_masks = [(lane // HEAD_DIM == hh) for hh in range(nh)]
    scale = HEAD_DIM ** -0.5

    @pl.when(pl.program_id(1) == 0)
    def _():
        for hh in range(nh):
            g_rows = jnp.broadcast_to(g_ref[hh], (BAND_QBLK, BAND_TBL + BAND_QBLK))
            tbl_ref[hh] = pltpu.roll(g_rows, 0, 1, stride=1, stride_axis=0)[:, BAND_QBLK:]
        row_chunk = lax.broadcasted_iota(jnp.int32, (BAND_QBLK, BAND_WIN), 0) // CHUNK
        col_chunk = lax.broadcasted_iota(jnp.int32, (BAND_QBLK, BAND_WIN), 1) // CHUNK
        for slot in range(n_head + 1):
            first_key_chunk_minus_query_chunk = -min(slot * chunks_per_blk, BAND_CHUNKS)
            tbl_off = lead + first_key_chunk_minus_query_chunk * CHUNK
            rel = col_chunk + first_key_chunk_minus_query_chunk - row_chunk
            valid = (rel <= 0) & (rel >= -BAND_CHUNKS)
            for hh in range(nh):
                bias_ref[slot * nh + hh] = jnp.where(
                    valid, tbl_ref[hh, :, tbl_off:tbl_off + BAND_WIN], NEG_BIG)

    def q_block(q0, k0, slot):
        q = q_ref[0, pl.ds(q0, BAND_QBLK), :] * scale
        zero = jnp.zeros_like(q)
        kb = k_ref[0, pl.ds(k0, BAND_WIN), :]
        vb = v_ref[0, pl.ds(k0, BAND_WIN), :]
        qh = jnp.concatenate([jnp.where(head_masks[hh], q, zero) for hh in range(nh)], axis=0)
        sc = lax.dot_general(qh, kb, (((1,), (1,)), ((), ())), preferred_element_type=F32)
        sc = sc + jnp.concatenate([bias_ref[slot * nh + hh] for hh in range(nh)], axis=0)
        m = jnp.max(sc, axis=-1, keepdims=True)
        e = jnp.exp(sc - m)
        denom = jnp.sum(e, axis=-1, keepdims=True)
        oh = jnp.dot(e.astype(BF16), vb, preferred_element_type=F32) / denom
        out = oh[0:BAND_QBLK]
        for hh in range(1, nh):
            out = jnp.where(head_masks[hh], oh[hh * BAND_QBLK:(hh + 1) * BAND_QBLK], out)
        o_ref[0, pl.ds(q0, BAND_QBLK), :] = out.astype(BF16)

    for i in range(n_head):
        q_block(i * BAND_QBLK, 0, i)

    n_general = s // BAND_QBLK - n_head
    group = math.gcd(n_general, BAND_GROUP)

    def body(t, _):
        for u in range(group):
            q0 = pl.multiple_of((n_head + t * group + u) * BAND_QBLK, BAND_QBLK)
            q_block(q0, pl.multiple_of(q0 - lead, BAND_QBLK), n_head)
        return 0

    lax.fori_loop(0, n_general // group, body, 0)


def _band_bias_vector(rel_bias):
    lead = BAND_CHUNKS * CHUNK
    span = BAND_TBL + BAND_QBLK
    n_tab = 2 * REL_CLIP + 1
    n_hi = BAND_QBLK + lead - REL_CLIP
    g = jnp.concatenate([jnp.repeat(rel_bias[:, -1:], n_hi, axis=1), rel_bias[:, ::-1],
                         jnp.repeat(rel_bias[:, :1], span - n_hi - n_tab, axis=1)], axis=1)
    return g.reshape(-1, 1, span).astype(F32)


def _band_attention(qkv, rel_bias):
    b, s, _ = qkv.shape
    np_ = N_HEAD_PAIRS
    g = _band_bias_vector(rel_bias)
    return pl.pallas_call(
        _band_kernel,
        grid=(np_, b),
        in_specs=[
            pl.BlockSpec((1, s, LANES), lambda p, bi: (bi, 0, p)),
            pl.BlockSpec((1, s, LANES), lambda p, bi: (bi, 0, np_ + p)),
            pl.BlockSpec((1, s, LANES), lambda p, bi: (bi, 0, 2 * np_ + p)),
            pl.BlockSpec((HEADS_PER_LANE_TILE, 1, BAND_TBL + BAND_QBLK), lambda p, bi: (p, 0, 0)),
        ],
        out_specs=pl.BlockSpec((1, s, LANES), lambda p, bi: (bi, 0, p)),
        out_shape=jax.ShapeDtypeStruct((b, s, D_MODEL), BF16),
        scratch_shapes=[
            pltpu.VMEM((HEADS_PER_LANE_TILE, BAND_QBLK, BAND_TBL), F32),
            pltpu.VMEM(((BAND_CHUNKS * CHUNK // BAND_QBLK + 1) * HEADS_PER_LANE_TILE,
                        BAND_QBLK, BAND_WIN), F32),
        ],
        compiler_params=_cparams("parallel", "arbitrary"),
        name="band_attention",
    )(qkv, qkv, qkv, g)


def kernel(x, c, ada_w, ada_b, norm_g, ffn_w_in, ffn_w_out, sb_w_qkv, sb_w_o, sg_w_in, sg_ln_g, sg_w_s, sg_bias, sg_w_out, sc_w_in, sc_conv_w, sc_w_out, cb_w_qkv, cb_rel_bias, cb_w_o):
    assert x.shape[1] % ROW_TILE == 0 and x.shape[2] == D_MODEL
    mod_all = _ada_modulation(c, ada_w, ada_b)
    for i in range(DEPTH):
        m, r = i % N_MIXERS, i // N_MIXERS
        mod = mod_all[i]
        g_mix = norm_g[i, 0:2]
        attn_out = w_o = None
        if m == 0:
            qkv = _norm_mod_proj(x, mod, norm_g[i, 0], sb_w_qkv[r], 0)
            attn_out, w_o = _stick_breaking_attention(qkv), sb_w_o[r]
        elif m == 1:
            x = _sgu_sublayer(x, mod, g_mix, sg_w_in[r], sg_ln_g[r], sg_w_s[r], sg_bias[r],
                              sg_w_out[r])
        elif m == 2:
            x = _conv_sublayer(x, mod, g_mix, sc_w_in[r], sc_conv_w[r], sc_w_out[r])
        else:
            qkv = _norm_mod_proj(x, mod, norm_g[i, 0], cb_w_qkv[r], 0)
            attn_out, w_o = _band_attention(qkv, cb_rel_bias[r]), cb_w_o[r]
        x = _ffn_sublayer(x, mod, norm_g[i], ffn_w_in, ffn_w_out, i, attn_out, w_o)
    return x
```

```python
import functools
import math

import jax
import jax.numpy as jnp
from jax import lax
from jax.experimental import pallas as pl
from jax.experimental.pallas import tpu as pltpu

D_MODEL = 1024
DEPTH = 4
N_MIXERS = 4
N_HEADS = 16
HEAD_DIM = D_MODEL // N_HEADS
CHUNK = 64
GMLP_BLOCK = 128
GMLP_WIDTH = 2 * D_MODEL
GMLP_GROUPS = 8
GMLP_GROUP_CH = GMLP_WIDTH // GMLP_GROUPS
CONV_WIDTH = 3
BAND_CHUNKS = 8
REL_CLIP = 128
D_FF = 2816
EPS = 1e-6

LANES = 128
HEADS_PER_LANE_TILE = LANES // HEAD_DIM
N_HEAD_PAIRS = N_HEADS // HEADS_PER_LANE_TILE
V7X_VMEM_BYTES = 64 * 1024 * 1024
VMEM_LIMIT = V7X_VMEM_BYTES - 8 * 1024 * 1024
FFN_VMEM_LIMIT = V7X_VMEM_BYTES - 4 * 1024 * 1024

ROW_TILE = 512
FFN_CHUNK = 256
SB_QBLK = 128
SB_KBLK = 256
SB_EXP_ZERO = 105.0
SB_SLACK_PER_SCORE = 2.0 ** -6
SB_SLACK_CONST = 0.1
BAND_QBLK = 4 * CHUNK
BAND_GROUP = 14
BAND_WIN = BAND_CHUNKS * CHUNK + BAND_QBLK
BAND_TBL = BAND_WIN + BAND_CHUNKS * CHUNK
NEG_BIG = -1e30

BF16 = jnp.bfloat16
F32 = jnp.float32


def _cparams(*sem):
    return pltpu.CompilerParams(dimension_semantics=sem, vmem_limit_bytes=VMEM_LIMIT)


def _resident(shape):
    zeros = (0,) * len(shape)
    return pl.BlockSpec(shape, lambda *_: zeros, pipeline_mode=pl.Buffered(1))


def _resident_layer(shape, layer):
    index = (layer,) + (0,) * len(shape)
    return pl.BlockSpec((pl.Squeezed(),) + tuple(shape), lambda *_: index,
                        pipeline_mode=pl.Buffered(1))


def _rms_scale(v):
    return lax.rsqrt(jnp.mean(v * v, axis=-1, keepdims=True) + EPS)


def _prologue(x, g, shift, scale):
    y = x * _rms_scale(x) * g
    return (y * (1.0 + scale) + shift).astype(BF16)


def _epilogue(x, y, g, gate):
    return x + gate * (y * _rms_scale(y) * g)


def _ada_kernel(c_ref, w_ref, b_ref, o_ref):
    c = c_ref[...]
    a = (c * jax.nn.sigmoid(c)).astype(BF16)
    w = w_ref[0].astype(BF16)
    o_ref[0] = jnp.dot(a, w, preferred_element_type=F32) + b_ref[0]


def _ada_modulation(c, ada_w, ada_b):
    b = c.shape[0]
    rows = 8
    n = ada_w.shape[-1]
    tn = 1536
    c_pad = jnp.zeros((rows, D_MODEL), F32).at[:b].set(c)
    out = pl.pallas_call(
        _ada_kernel,
        grid=(DEPTH, n // tn),
        in_specs=[
            pl.BlockSpec((rows, D_MODEL), lambda l, j: (0, 0)),
            pl.BlockSpec((1, D_MODEL, tn), lambda l, j: (l, 0, j)),
            pl.BlockSpec((1, 1, tn), lambda l, j: (l, 0, j)),
        ],
        out_specs=pl.BlockSpec((1, rows, tn), lambda l, j: (l, 0, j)),
        out_shape=jax.ShapeDtypeStruct((DEPTH, rows, n), F32),
        compiler_params=_cparams("parallel", "parallel"),
        name="ada_modulation",
    )(c_pad, ada_w, ada_b.reshape(DEPTH, 1, n))
    return out[:, :b].reshape(DEPTH, b, 6, D_MODEL)


def _proj_kernel(x_ref, mod_ref, g_ref, w_ref, o_ref, *, mod_row, n_chunk):
    h = _prologue(x_ref[0], g_ref[...], mod_ref[0, mod_row:mod_row + 1, :],
                  mod_ref[0, mod_row + 1:mod_row + 2, :])
    n = w_ref.shape[1]
    for c in range(n // n_chunk):
        sl = slice(c * n_chunk, (c + 1) * n_chunk)
        o_ref[0, :, sl] = jnp.dot(h, w_ref[:, sl].astype(BF16),
                                  preferred_element_type=F32).astype(BF16)


def _norm_mod_proj(x, mod, g, w, mod_row):
    b, s, d = x.shape
    n = w.shape[1]
    return pl.pallas_call(
        functools.partial(_proj_kernel, mod_row=mod_row, n_chunk=512),
        grid=(b, s // ROW_TILE),
        in_specs=[
            pl.BlockSpec((1, ROW_TILE, d), lambda bi, i: (bi, i, 0)),
            pl.BlockSpec((1, 6, d), lambda bi, i: (bi, 0, 0)),
            _resident((1, d)),
            _resident((d, n)),
        ],
        out_specs=pl.BlockSpec((1, ROW_TILE, n), lambda bi, i: (bi, i, 0)),
        out_shape=jax.ShapeDtypeStruct((b, s, n), BF16),
        compiler_params=_cparams("parallel", "parallel"),
        name="norm_mod_proj",
    )(x, mod, g.reshape(1, d), w)


def _ffn_body(x, mod_ref, g_ref, win_ref, wout_ref):
    h = _prologue(x, g_ref[2:3, :], mod_ref[0, 3:4, :], mod_ref[0, 4:5, :])
    for c0 in range(0, D_FF, FFN_CHUNK):
        w = min(FFN_CHUNK, D_FF - c0)
        gate = jnp.dot(h, win_ref[:, c0:c0 + w].astype(BF16), preferred_element_type=F32)
        up = jnp.dot(h, win_ref[:, D_FF + c0:D_FF + c0 + w].astype(BF16),
                     preferred_element_type=F32)
        a = (gate * jax.nn.sigmoid(gate) * up).astype(BF16)
        part = jnp.dot(a, wout_ref[c0:c0 + w, :].astype(BF16), preferred_element_type=F32)
        y = part if c0 == 0 else y + part
    return _epilogue(x, y, g_ref[3:4, :], mod_ref[0, 5:6, :])


def _ffn_kernel(x_ref, mod_ref, g_ref, win_ref, wout_ref, o_ref):
    o_ref[0] = _ffn_body(x_ref[0], mod_ref, g_ref, win_ref, wout_ref)


def _proj_ffn_kernel(a_ref, x_ref, mod_ref, g_ref, wo_ref, win_ref, wout_ref, o_ref, mid_ref):
    y = jnp.dot(a_ref[0], wo_ref[...], preferred_element_type=F32)
    mid_ref[...] = _epilogue(x_ref[0], y, g_ref[1:2, :], mod_ref[0, 2:3, :])
    o_ref[0] = _ffn_body(mid_ref[...], mod_ref, g_ref, win_ref, wout_ref)


def _ffn_sublayer(x, mod, g4, w_in, w_out, layer, attn_out=None, w_o=None):
    b, s, d = x.shape
    row_spec = pl.BlockSpec((1, ROW_TILE, d), lambda bi, i: (bi, i, 0))
    common = [
        pl.BlockSpec((1, 6, d), lambda bi, i: (bi, 0, 0)),
        _resident((4, d)),
    ]
    weights = [_resident_layer((d, 2 * D_FF), layer), _resident_layer((D_FF, d), layer)]
    if attn_out is None:
        body, in_specs, args = _ffn_kernel, [row_spec] + common + weights, (x, mod, g4, w_in, w_out)
        scratch = []
    else:
        body = _proj_ffn_kernel
        in_specs = [row_spec, row_spec] + common + [_resident((d, d))] + weights
        args = (attn_out, x, mod, g4, w_o.astype(BF16), w_in, w_out)
        scratch = [pltpu.VMEM((ROW_TILE, d), F32)]
    return pl.pallas_call(
        body,
        grid=(b, s // ROW_TILE),
        in_specs=in_specs,
        out_specs=row_spec,
        out_shape=jax.ShapeDtypeStruct((b, s, d), F32),
        scratch_shapes=scratch,
        compiler_params=pltpu.CompilerParams(
            dimension_semantics=("parallel", "parallel"), vmem_limit_bytes=FFN_VMEM_LIMIT),
        name="ffn_sublayer" if attn_out is None else "proj_ffn_sublayer",
    )(*args)


def _gelu_tanh(x):
    c = 0.7978845608028654
    return 0.5 * x * (1.0 + jnp.tanh(c * (x + 0.044715 * (x * x * x))))


def _sgu_kernel(x_ref, mod_ref, g_ref, win_ref, lng_ref, ws_ref, sb_ref, wout_ref,
                o_ref, u_ref, v_ref, y_ref):
    x = x_ref[0]
    h = _prologue(x, g_ref[0:1, :], mod_ref[0, 0:1, :], mod_ref[0, 1:2, :])
    n_chunk = 512
    half = GMLP_WIDTH // n_chunk
    for c in range(half):
        sl = slice(GMLP_WIDTH + c * n_chunk, GMLP_WIDTH + (c + 1) * n_chunk)
        v_ref[:, c * n_chunk:(c + 1) * n_chunk] = _gelu_tanh(
            jnp.dot(h, win_ref[:, sl].astype(BF16), preferred_element_type=F32))
    for c in range(half):
        sl = slice(c * n_chunk, (c + 1) * n_chunk)
        u_ref[:, sl] = _gelu_tanh(
            jnp.dot(h, win_ref[:, sl].astype(BF16), preferred_element_type=F32))
    v = v_ref[...]
    mu = jnp.mean(v, axis=-1, keepdims=True)
    vc = v - mu
    var = jnp.mean(vc * vc, axis=-1, keepdims=True)
    vn = (vc * lax.rsqrt(var + EPS) * lng_ref[...]).astype(BF16)
    r = lax.broadcasted_iota(jnp.int32, (GMLP_BLOCK, GMLP_BLOCK), 0) // CHUNK
    cidx = lax.broadcasted_iota(jnp.int32, (GMLP_BLOCK, GMLP_BLOCK), 1) // CHUNK
    keep = r >= cidx
    y = None
    for gi in range(GMLP_GROUPS):
        cs = slice(gi * GMLP_GROUP_CH, (gi + 1) * GMLP_GROUP_CH)
        ws = jnp.where(keep, ws_ref[gi], 0.0).astype(BF16)
        bias = sb_ref[:, gi:gi + 1]
        for blk in range(ROW_TILE // GMLP_BLOCK):
            rs = slice(blk * GMLP_BLOCK, (blk + 1) * GMLP_BLOCK)
            sv = jnp.dot(ws, vn[rs, cs], preferred_element_type=F32) + bias
            y_ref[rs, cs] = (u_ref[rs, cs] * sv).astype(BF16)
        if gi % 2:
            ks = slice((gi - 1) * GMLP_GROUP_CH, (gi + 1) * GMLP_GROUP_CH)
            part = jnp.dot(y_ref[:, ks], wout_ref[ks, :].astype(BF16),
                           preferred_element_type=F32)
            y = part if y is None else y + part
    o_ref[0] = _epilogue(x, y, g_ref[1:2, :], mod_ref[0, 2:3, :])


def _sgu_sublayer(x, mod, g2, w_in, ln_g, w_s, s_bias, w_out):
    b, s, d = x.shape
    return pl.pallas_call(
        _sgu_kernel,
        grid=(b, s // ROW_TILE),
        in_specs=[
            pl.BlockSpec((1, ROW_TILE, d), lambda bi, i: (bi, i, 0)),
            pl.BlockSpec((1, 6, d), lambda bi, i: (bi, 0, 0)),
            _resident((2, d)),
            _resident((d, 2 * GMLP_WIDTH)),
            _resident((1, GMLP_WIDTH)),
            _resident((GMLP_GROUPS, GMLP_BLOCK, GMLP_BLOCK)),
            _resident((GMLP_BLOCK, GMLP_GROUPS)),
            _resident((GMLP_WIDTH, d)),
        ],
        out_specs=pl.BlockSpec((1, ROW_TILE, d), lambda bi, i: (bi, i, 0)),
        out_shape=jax.ShapeDtypeStruct((b, s, d), F32),
        scratch_shapes=[pltpu.VMEM((ROW_TILE, GMLP_WIDTH), F32),
                        pltpu.VMEM((ROW_TILE, GMLP_WIDTH), F32),
                        pltpu.VMEM((ROW_TILE, GMLP_WIDTH), BF16)],
        compiler_params=_cparams("parallel", "parallel"),
        name="sgu_sublayer",
    )(x, mod, g2, w_in, ln_g.reshape(1, GMLP_WIDTH), w_s, s_bias.T, w_out)


CONV_HALO = 8


def _conv_kernel(x_ref, mod_ref, g_ref, win_ref, cw_ref, wout_ref, o_ref, ext_ref, t_ref):
    i = pl.program_id(1)
    x = x_ref[0]
    h = _prologue(x, g_ref[0:1, :], mod_ref[0, 0:1, :], mod_ref[0, 1:2, :])

    @pl.when(i == 0)
    def _():
        ext_ref[0:CONV_HALO, :] = jnp.zeros((CONV_HALO, D_MODEL), F32)

    @pl.when(i != 0)
    def _():
        ext_ref[0:CONV_HALO, :] = ext_ref[ROW_TILE:ROW_TILE + CONV_HALO, :]

    n_chunk = 512
    for c in range(D_MODEL // n_chunk):
        cs = slice(c * n_chunk, (c + 1) * n_chunk)
        gate_c = jnp.dot(h, win_ref[:, D_MODEL + c * n_chunk:D_MODEL + (c + 1) * n_chunk]
                         .astype(BF16), preferred_element_type=F32)
        xt = jnp.dot(h, win_ref[:, 2 * D_MODEL + c * n_chunk:2 * D_MODEL + (c + 1) * n_chunk]
                     .astype(BF16), preferred_element_type=F32)
        ext_ref[CONV_HALO:, cs] = gate_c * xt
    for c in range(D_MODEL // n_chunk):
        cs = slice(c * n_chunk, (c + 1) * n_chunk)
        gate_b = jnp.dot(h, win_ref[:, cs].astype(BF16), preferred_element_type=F32)
        yc = (cw_ref[0:1, cs] * ext_ref[CONV_HALO - 2:CONV_HALO - 2 + ROW_TILE, cs]
              + cw_ref[1:2, cs] * ext_ref[CONV_HALO - 1:CONV_HALO - 1 + ROW_TILE, cs]
              + cw_ref[2:3, cs] * ext_ref[CONV_HALO:, cs])
        t_ref[:, cs] = (gate_b * yc).astype(BF16)
    y = jnp.dot(t_ref[...], wout_ref[...].astype(BF16), preferred_element_type=F32)
    o_ref[0] = _epilogue(x, y, g_ref[1:2, :], mod_ref[0, 2:3, :])


def _conv_sublayer(x, mod, g2, w_in, conv_w, w_out):
    b, s, d = x.shape
    return pl.pallas_call(
        _conv_kernel,
        grid=(b, s // ROW_TILE),
        in_specs=[
            pl.BlockSpec((1, ROW_TILE, d), lambda bi, i: (bi, i, 0)),
            pl.BlockSpec((1, 6, d), lambda bi, i: (bi, 0, 0)),
            _resident((2, d)),
            _resident((d, 3 * d)),
            _resident((CONV_WIDTH, d)),
            _resident((d, d)),
        ],
        out_specs=pl.BlockSpec((1, ROW_TILE, d), lambda bi, i: (bi, i, 0)),
        out_shape=jax.ShapeDtypeStruct((b, s, d), F32),
        scratch_shapes=[pltpu.VMEM((ROW_TILE + CONV_HALO, d), F32),
                        pltpu.VMEM((ROW_TILE, d), BF16)],
        compiler_params=_cparams("parallel", "arbitrary"),
        name="conv_sublayer",
    )(x, mod, g2, w_in, conv_w, w_out)


def _sb_kernel(q_ref, k_ref, v_ref, o_ref, qm_ref, acc_ref, run_ref, slack_ref):
    s = q_ref.shape[1]
    nh = HEADS_PER_LANE_TILE
    n_chunks = s // SB_QBLK
    rows_per_chunk = SB_QBLK * nh
    n_steps = (n_chunks - 1) // 2 + 1
    lane = lax.broadcasted_iota(jnp.int32, (1, LANES), 1)
    head_masks = [(lane // HEAD_DIM == hh) for hh in range(nh)]
    tri = jnp.where(lax.broadcasted_iota(jnp.int32, (SB_KBLK, SB_KBLK), 0)
                    >= lax.broadcasted_iota(jnp.int32, (SB_KBLK, SB_KBLK), 1),
                    1.0, 0.0).astype(BF16)
    row = lax.broadcasted_iota(jnp.int32, (SB_QBLK, SB_KBLK), 0)
    col = lax.broadcasted_iota(jnp.int32, (SB_QBLK, SB_KBLK), 1)
    key_in_first_half = lax.broadcasted_iota(jnp.int32, (SB_KBLK, 1), 0) < SB_QBLK
    col_in_first_half = col < SB_QBLK
    scale = HEAD_DIM ** -0.5
    big = jnp.full((8, LANES), jnp.inf, F32)

    kf = k_ref[0].astype(F32)
    k_norm_max = jnp.sqrt(jnp.max(jnp.sum(kf * kf, axis=-1, keepdims=True),
                                  axis=0, keepdims=True))

    def setup(c):
        q = q_ref[0, c * SB_QBLK:(c + 1) * SB_QBLK, :] * scale
        for hh in range(nh):
            r0 = c * rows_per_chunk + hh * SB_QBLK
            qh = jnp.where(head_masks[hh], q, jnp.zeros_like(q))
            qm_ref[r0:r0 + SB_QBLK, :] = qh
            qf = qh.astype(F32)
            z_bound = jnp.sqrt(jnp.sum(qf * qf, axis=-1, keepdims=True)) * k_norm_max
            slack_ref[r0:r0 + SB_QBLK, :] = jnp.broadcast_to(
                z_bound * SB_SLACK_PER_SCORE + SB_SLACK_CONST, (SB_QBLK, LANES))
        rs = slice(c * rows_per_chunk, (c + 1) * rows_per_chunk)
        acc_ref[rs, :] = jnp.zeros((rows_per_chunk, LANES), F32)
        run_ref[rs, :] = jnp.zeros((rows_per_chunk, LANES), F32)

    def step(d):
        rows = [slice(hh * SB_QBLK, (hh + 1) * SB_QBLK) for hh in range(nh)]
        per_chunk, sps = [], []
        for c in range(n_chunks):
            rs = slice(c * rows_per_chunk, (c + 1) * rows_per_chunk)
            if d is None:
                k0 = max(c * SB_QBLK - SB_QBLK, 0)
                mask = col < row + (c * SB_QBLK - k0)
                active = clipped = None
                more_keys = c >= 2
            else:
                nominal = c * SB_QBLK - SB_QBLK - d * SB_KBLK
                k0 = pl.multiple_of(jnp.maximum(nominal, 0), SB_QBLK)
                mask = None
                active = c >= 2 * d
                more_keys = c >= 2 * d + 2
                clipped = None if c % 2 else 2 * d == c
            kb = k_ref[0, pl.ds(k0, SB_KBLK), :]
            vb = v_ref[0, pl.ds(k0, SB_KBLK), :]
            if clipped is not None:
                key_ok = key_in_first_half | jnp.logical_not(clipped)
                kb = jnp.where(key_ok, kb, jnp.zeros_like(kb))
                vb = jnp.where(key_ok, vb, jnp.zeros_like(vb))
                mask = col_in_first_half | jnp.logical_not(clipped)
            z = lax.dot_general(qm_ref[rs, :], kb, (((1,), (1,)), ((), ())),
                                preferred_element_type=F32)
            for r in rows:
                zb = z[r].astype(BF16)
                sp = jnp.maximum(zb, 0.0) + jnp.log(1.0 + jnp.exp(-jnp.abs(zb)))
                if mask is not None:
                    sp = jnp.where(mask, sp, jnp.zeros_like(sp))
                sps.append(sp)
            per_chunk.append((rs, z, vb, mask if d is None else None, active, more_keys))
        rest_all = jnp.dot(jnp.concatenate(sps, axis=0), tri, preferred_element_type=F32)
        gap = big
        for c, (rs, z, vb, mask, active, more_keys) in enumerate(per_chunk):
            rest = rest_all[rs]
            run = run_ref[rs, :]
            a_s = []
            for r in rows:
                run_c = jnp.concatenate([run[r]] * (SB_KBLK // LANES), axis=1)
                a = jnp.exp(z[r] - rest[r] - run_c)
                if mask is not None:
                    a = jnp.where(mask, a, 0.0)
                a_s.append(a.astype(BF16))
            part = jnp.dot(jnp.concatenate(a_s, axis=0), vb, preferred_element_type=F32)
            total = jnp.broadcast_to(rest[:, 0:1], run.shape)
            if active is not None:
                part = jnp.where(active, part, 0.0)
                total = jnp.where(active, total, 0.0)
            acc_ref[rs, :] += part
            run = run + total
            run_ref[rs, :] = run
            if more_keys is False:
                continue
            chunk_gap = jnp.min((run - slack_ref[rs, :]).reshape(-1, 8, LANES), axis=0)
            if more_keys is not True:
                chunk_gap = jnp.where(more_keys, chunk_gap, big)
            gap = jnp.minimum(gap, chunk_gap)
        return gap

    def margin(gap):
        return jnp.min(gap, axis=0, keepdims=True)[0, 0]

    for c in range(n_chunks):
        setup(c)
    gap0 = step(None)

    def more_keys_matter(carry):
        d, mg = carry
        return (d < n_steps) & (mg < SB_EXP_ZERO)

    def walk(carry):
        d, _ = carry
        return d + 1, margin(step(d))

    lax.while_loop(more_keys_matter, walk, (jnp.int32(1), margin(gap0)))

    for c in range(n_chunks):
        r0 = c * rows_per_chunk
        out = acc_ref[r0:r0 + SB_QBLK, :]
        for hh in range(1, nh):
            out = jnp.where(head_masks[hh],
                            acc_ref[r0 + hh * SB_QBLK:r0 + (hh + 1) * SB_QBLK, :], out)
        o_ref[0, c * SB_QBLK:(c + 1) * SB_QBLK, :] = out.astype(BF16)


def _stick_breaking_attention(qkv):
    b, s, _ = qkv.shape
    np_ = N_HEAD_PAIRS
    assert s % SB_KBLK == 0
    rows = s * HEADS_PER_LANE_TILE
    return pl.pallas_call(
        _sb_kernel,
        grid=(b, np_),
        in_specs=[
            pl.BlockSpec((1, s, LANES), lambda bi, p: (bi, 0, p)),
            pl.BlockSpec((1, s, LANES), lambda bi, p: (bi, 0, np_ + p)),
            pl.BlockSpec((1, s, LANES), lambda bi, p: (bi, 0, 2 * np_ + p)),
        ],
        out_specs=pl.BlockSpec((1, s, LANES), lambda bi, p: (bi, 0, p)),
        out_shape=jax.ShapeDtypeStruct((b, s, D_MODEL), BF16),
        scratch_shapes=[
            pltpu.VMEM((rows, LANES), BF16),
            pltpu.VMEM((rows, LANES), F32),
            pltpu.VMEM((rows, LANES), F32),
            pltpu.VMEM((rows, LANES), F32),
        ],
        compiler_params=_cparams("parallel", "parallel"),
        name="stick_breaking_attention",
    )(qkv, qkv, qkv)


def _band_kernel(q_ref, k_ref, v_ref, g_ref, o_ref, tbl_ref, bias_ref):
    s = q_ref.shape[1]
    lead = BAND_CHUNKS * CHUNK
    nh = HEADS_PER_LANE_TILE
    n_head = lead // BAND_QBLK
    chunks_per_blk = BAND_QBLK // CHUNK
    lane = lax.broadcasted_iota(jnp.int32, (1, LANES), 1)
    head_masks = [(lane // HEAD_DIM == hh) for hh in range(nh)]
    scale = HEAD_DIM ** -0.5

    @pl.when(pl.program_id(1) == 0)
    def _():
        for hh in range(nh):
            g_rows = jnp.broadcast_to(g_ref[hh], (BAND_QBLK, BAND_TBL + BAND_QBLK))
            tbl_ref[hh] = pltpu.roll(g_rows, 0, 1, stride=1, stride_axis=0)[:, BAND_QBLK:]
        row_chunk = lax.broadcasted_iota(jnp.int32, (BAND_QBLK, BAND_WIN), 0) // CHUNK
        col_chunk = lax.broadcasted_iota(jnp.int32, (BAND_QBLK, BAND_WIN), 1) // CHUNK
        for slot in range(n_head + 1):
            first_key_chunk_minus_query_chunk = -min(slot * chunks_per_blk, BAND_CHUNKS)
            tbl_off = lead + first_key_chunk_minus_query_chunk * CHUNK
            rel = col_chunk + first_key_chunk_minus_query_chunk - row_chunk
            valid = (rel <= 0) & (rel >= -BAND_CHUNKS)
            for hh in range(nh):
                bias_ref[slot * nh + hh] = jnp.where(
                    valid, tbl_ref[hh, :, tbl_off:tbl_off + BAND_WIN], NEG_BIG)

    def q_block(q0, k0, slot):
        q = q_ref[0, pl.ds(q0, BAND_QBLK), :] * scale
        zero = jnp.zeros_like(q)
        kb = k_ref[0, pl.ds(k0, BAND_WIN), :]
        vb = v_ref[0, pl.ds(k0, BAND_WIN), :]
        qh = jnp.concatenate([jnp.where(head_masks[hh], q, zero) for hh in range(nh)], axis=0)
        sc = lax.dot_general(qh, kb, (((1,), (1,)), ((), ())), preferred_element_type=F32)
        sc = sc + jnp.concatenate([bias_ref[slot * nh + hh] for hh in range(nh)], axis=0)
        m = jnp.max(sc, axis=-1, keepdims=True)
        e = jnp.exp(sc - m)
        denom = jnp.sum(e, axis=-1, keepdims=True)
        oh = jnp.dot(e.astype(BF16), vb, preferred_element_type=F32) / denom
        out = oh[0:BAND_QBLK]
        for hh in range(1, nh):
            out = jnp.where(head_masks[hh], oh[hh * BAND_QBLK:(hh + 1) * BAND_QBLK], out)
        o_ref[0, pl.ds(q0, BAND_QBLK), :] = out.astype(BF16)

    for i in range(n_head):
        q_block(i * BAND_QBLK, 0, i)

    n_general = s // BAND_QBLK - n_head
    group = math.gcd(n_general, BAND_GROUP)

    def body(t, _):
        for u in range(group):
            q0 = pl.multiple_of((n_head + t * group + u) * BAND_QBLK, BAND_QBLK)
            q_block(q0, pl.multiple_of(q0 - lead, BAND_QBLK), n_head)
        return 0

    lax.fori_loop(0, n_general // group, body, 0)


def _band_bias_vector(rel_bias):
    lead = BAND_CHUNKS * CHUNK
    span = BAND_TBL + BAND_QBLK
    n_tab = 2 * REL_CLIP + 1
    n_hi = BAND_QBLK + lead - REL_CLIP
    g = jnp.concatenate([jnp.repeat(rel_bias[:, -1:], n_hi, axis=1), rel_bias[:, ::-1],
                         jnp.repeat(rel_bias[:, :1], span - n_hi - n_tab, axis=1)], axis=1)
    return g.reshape(-1, 1, span).astype(F32)


def _band_attention(qkv, rel_bias):
    b, s, _ = qkv.shape
    np_ = N_HEAD_PAIRS
    g = _band_bias_vector(rel_bias)
    return pl.pallas_call(
        _band_kernel,
        grid=(np_, b),
        in_specs=[
            pl.BlockSpec((1, s, LANES), lambda p, bi: (bi, 0, p)),
            pl.BlockSpec((1, s, LANES), lambda p, bi: (bi, 0, np_ + p)),
            pl.BlockSpec((1, s, LANES), lambda p, bi: (bi, 0, 2 * np_ + p)),
            pl.BlockSpec((HEADS_PER_LANE_TILE, 1, BAND_TBL + BAND_QBLK), lambda p, bi: (p, 0, 0)),
        ],
        out_specs=pl.BlockSpec((1, s, LANES), lambda p, bi: (bi, 0, p)),
        out_shape=jax.ShapeDtypeStruct((b, s, D_MODEL), BF16),
        scratch_shapes=[
            pltpu.VMEM((HEADS_PER_LANE_TILE, BAND_QBLK, BAND_TBL), F32),
            pltpu.VMEM(((BAND_CHUNKS * CHUNK // BAND_QBLK + 1) * HEADS_PER_LANE_TILE,
                        BAND_QBLK, BAND_WIN), F32),
        ],
        compiler_params=_cparams("parallel", "arbitrary"),
        name="band_attention",
    )(qkv, qkv, qkv, g)


def kernel(x, c, ada_w, ada_b, norm_g, ffn_w_in, ffn_w_out, sb_w_qkv, sb_w_o, sg_w_in, sg_ln_g, sg_w_s, sg_bias, sg_w_out, sc_w_in, sc_conv_w, sc_w_out, cb_w_qkv, cb_rel_bias, cb_w_o):
    assert x.shape[1] % ROW_TILE == 0 and x.shape[2] == D_MODEL
    mod_all = _ada_modulation(c, ada_w, ada_b)
    for i in range(DEPTH):
        m, r = i % N_MIXERS, i // N_MIXERS
        mod = mod_all[i]
        g_mix = norm_g[i, 0:2]
        attn_out = w_o = None
        if m == 0:
            qkv = _norm_mod_proj(x, mod, norm_g[i, 0], sb_w_qkv[r], 0)
            attn_out, w_o = _stick_breaking_attention(qkv), sb_w_o[r]
        elif m == 1:
            x = _sgu_sublayer(x, mod, g_mix, sg_w_in[r], sg_ln_g[r], sg_w_s[r], sg_bias[r],
                              sg_w_out[r])
        elif m == 2:
            x = _conv_sublayer(x, mod, g_mix, sc_w_in[r], sc_conv_w[r], sc_w_out[r])
        else:
            qkv = _norm_mod_proj(x, mod, norm_g[i, 0], cb_w_qkv[r], 0)
            attn_out, w_o = _band_attention(qkv, cb_rel_bias[r]), cb_w_o[r]
        x = _ffn_sublayer(x, mod, norm_g[i], ffn_w_in, ffn_w_out, i, attn_out, w_o)
    return x
```
